```python
import math
import jax
import jax.numpy as jnp
from jax import lax
import numpy as np

D_MODEL = 4096
BATCH = 8
SEQ = 2048
DEPTH = 4

D_FF = 3072
ADA_RANK = 256
N_SUB = 3
RET_HEADS = 8
RET_QK_HD = 128
RET_V_HD = 256
RET_CHUNK = 128
RET_QK = RET_HEADS * RET_QK_HD
RET_V = RET_HEADS * RET_V_HD
ROPE_BASE = 10000.0
NSA_HEADS = 8
NSA_KV_HEADS = 2
NSA_HD = 128
NSA_Q = NSA_HEADS * NSA_HD
NSA_KV = NSA_KV_HEADS * NSA_HD
CMP_BLOCK = 32
CMP_STRIDE = 16
CMP_HIDDEN = 128
SEL_BLOCK = 64
N_SEL = 8
SEL_QBLK = 64
WINDOW = 512
WIN_QBLK = 128
RWKV_HD = 64
RWKV_DIM = 1024
RWKV_HEADS = RWKV_DIM // RWKV_HD
DECAY_LORA = 64
AAA_LORA = 64
GATE_LORA = 160
RWKV_LN_EPS = 64e-5
REL_BUCKETS = 32
REL_MAX_DIST = 128
RET_COLS = 2 * RET_QK + 2 * RET_V
NSA_COLS = NSA_Q + 6 * NSA_KV + 3 * NSA_HEADS
RWKV_COLS = 3 * RWKV_DIM + DECAY_LORA + AAA_LORA + GATE_LORA
IN_GROUPS = (3 * D_MODEL, RET_COLS, NSA_COLS, RWKV_COLS)
D_IN = 3 * D_MODEL + RET_COLS + NSA_COLS + RWKV_COLS
NEG = -1e30

kernel_name = "hybrid_retnet_nsa_rwkv7_macaron"


def split_last(z, sizes):
    out, start = [], 0
    for s in sizes:
        out.append(z[..., start:start + s])
        start += s
    return out


def rms_norm(x, g, eps=1e-6):
    xf = x.astype(jnp.float32)
    y = xf * lax.rsqrt(jnp.mean(xf * xf, axis=-1, keepdims=True) + eps)
    return (y * g.astype(jnp.float32)).astype(x.dtype)


def head_layer_norm(x, eps):
    xf = x.astype(jnp.float32)
    mu = jnp.mean(xf, axis=-1, keepdims=True)
    var = jnp.mean(jnp.square(xf - mu), axis=-1, keepdims=True)
    return (xf - mu) * lax.rsqrt(var + eps)


def masked_softmax(s, valid):
    p = jax.nn.softmax(jnp.where(valid, s.astype(jnp.float32), NEG), axis=-1)
    return jnp.where(valid, p, 0.0)


def rel_bucket(dist):
    n = jnp.maximum(dist, 0)
    max_exact = REL_BUCKETS // 2
    nf = jnp.maximum(n, 1).astype(jnp.float32)
    large = max_exact + (jnp.log(nf / max_exact) / math.log(REL_MAX_DIST / max_exact)
                         * (REL_BUCKETS - max_exact)).astype(jnp.int32)
    large = jnp.minimum(large, REL_BUCKETS - 1)
    return jnp.where(n < max_exact, n, large)


def swiglu(h, w_in, w_out):
    a, b = jnp.split(h @ w_in, 2, axis=-1)
    return (jax.nn.silu(a) * b) @ w_out


def modulate(h, gain, shift, scale):
    return rms_norm(h, gain) * (1 + scale[:, None, :]) + shift[:, None, :]


def gated_post(y, gain, gate):
    return gate[:, None, :] * rms_norm(y, gain)


def retention(z):
    B, S = z.shape[0], z.shape[1]
    H, C = RET_HEADS, RET_CHUNK
    q, k, v, g = split_last(z, (RET_QK, RET_QK, RET_V, RET_V))
    q = q.reshape(B, S, H, RET_QK_HD)
    k = k.reshape(B, S, H, RET_QK_HD)
    v = v.reshape(B, S, H, RET_V_HD)
    pos = jnp.arange(S, dtype=jnp.float32)
    inv = 1.0 / (ROPE_BASE ** jnp.linspace(0.0, 1.0, RET_QK_HD // 2))
    ang = pos[:, None] * inv[None, :]
    cos, sin = jnp.cos(ang)[:, None, :], jnp.sin(ang)[:, None, :]

    def rot(t):
        t1, t2 = jnp.split(t, 2, axis=-1)
        return jnp.concatenate([t1 * cos - t2 * sin, t2 * cos + t1 * sin], axis=-1).astype(t.dtype)

    q = rot(q)
    k = rot(k) * (RET_QK_HD ** -0.5)
    n_chunks = S // C

    def chunked(t):
        return t.reshape(B, n_chunks, C, H, t.shape[-1]).transpose(0, 3, 1, 2, 4)

    qc, kc, vc = chunked(q), chunked(k), chunked(v)
    log_g = jnp.log1p(-(2.0 ** (-5.0 - jnp.arange(H, dtype=jnp.float32))))
    idx = jnp.arange(C, dtype=jnp.float32)
    diff = idx[:, None] - idx[None, :]
    decay_mask = jnp.where(diff >= 0, jnp.exp(jnp.maximum(diff, 0.0)[None] * log_g[:, None, None]), 0.0)
    zeta = jnp.exp((C - 1 - idx)[None, :] * log_g[:, None])
    xi = jnp.exp((idx + 1)[None, :] * log_g[:, None])
    chunk_decay = jnp.exp(C * log_g)
    inner = jnp.einsum('bhnid,bhnjd->bhnij', qc, kc) * decay_mask[None, :, None]
    inner = jnp.einsum('bhnij,bhnje->bhnie', inner.astype(vc.dtype), vc)
    kv = jnp.einsum('bhnjd,bhnje->bhnde', kc * zeta[None, :, None, :, None].astype(kc.dtype), vc)

    def step(R, kv_i):
        return kv_i + chunk_decay[None, :, None, None] * R, R

    R0 = jnp.zeros((B, H, RET_QK_HD, RET_V_HD), jnp.float32)
    _, R_prev = lax.scan(step, R0, jnp.moveaxis(kv, 2, 0).astype(jnp.float32))
    R_prev = jnp.moveaxis(R_prev, 0, 2)
    cross = jnp.einsum('bhnid,bhnde->bhnie', qc.astype(jnp.float32), R_prev) * xi[None, :, None, :, None]
    o = head_layer_norm(inner.astype(jnp.float32) + cross, 1e-6)
    o = o.transpose(0, 2, 3, 1, 4).reshape(B, S, RET_V).astype(z.dtype)
    return jax.nn.silu(g) * o


def nsa(z, rel_bias, cmp_pos, cmp_w1, cmp_b1, cmp_w2, cmp_b2):
    B, S = z.shape[0], z.shape[1]
    G, HG, dh = NSA_KV_HEADS, NSA_HEADS // NSA_KV_HEADS, NSA_HD
    q, kv_all, gates = split_last(z, (NSA_Q, 6 * NSA_KV, 3 * NSA_HEADS))
    q = q.reshape(B, S, G, HG, dh) * (dh ** -0.5)
    k_c, v_c, k_s, v_s, k_w, v_w = [t.reshape(B, S, G, dh) for t in jnp.split(kv_all, 6, axis=-1)]
    gates = jax.nn.sigmoid(gates.reshape(B, S, G, HG, 3))
    bias_g = rel_bias.reshape(REL_BUCKETS, G, HG)
    qpos = jnp.arange(S)

    n_cmp = (S - CMP_BLOCK) // CMP_STRIDE + 1
    tok = CMP_STRIDE * jnp.arange(n_cmp)[:, None] + jnp.arange(CMP_BLOCK)[None, :]

    def compress(t, i):
        blk = t[:, tok] + cmp_pos[i][None, None, :, None, :]
        blk = blk.transpose(0, 1, 3, 2, 4).reshape(B, n_cmp, G, CMP_BLOCK * dh)
        return jax.nn.gelu(blk @ cmp_w1[i] + cmp_b1[i]) @ cmp_w2[i] + cmp_b2[i]

    kc, vc = compress(k_c, 0), compress(v_c, 1)
    cmp_end = tok[:, -1]
    dist_c = qpos[:, None] - cmp_end[None, :]
    valid_c = dist_c >= 0
    bias_c = bias_g[rel_bucket(dist_c)].transpose(2, 3, 0, 1)
    s_c = jnp.einsum('bsgjd,bngd->bgjsn', q, kc) + bias_c
    p_c = masked_softmax(s_c, valid_c)
    o_c = jnp.einsum('bgjsn,bngd->bsgjd', p_c.astype(vc.dtype), vc)

    n_sblk = S // SEL_BLOCK
    n_sel = min(N_SEL, n_sblk)
    sel_start = SEL_BLOCK * jnp.arange(n_sblk)
    overlap = (tok[:, 0][:, None] <= (sel_start + SEL_BLOCK - 1)[None, :]) & (cmp_end[:, None] >= sel_start[None, :])
    imp = jnp.einsum('bgjsn,nm->bgsm', p_c, overlap.astype(jnp.float32))
    cur = (qpos // SEL_BLOCK)[:, None]
    blk = jnp.arange(n_sblk)[None, :]
    forced = (blk == 0) | (blk == cur) | (blk == cur - 1)
    imp = jnp.where(forced, 1e30, jnp.where(blk <= cur, imp, NEG))
    _, sel_idx = lax.top_k(imp, n_sel)
    ks_blk = k_s.reshape(B, n_sblk, SEL_BLOCK, G, dh).transpose(0, 3, 1, 2, 4)
    vs_blk = v_s.reshape(B, n_sblk, SEL_BLOCK, G, dh).transpose(0, 3, 1, 2, 4)
    n_qb = S // SEL_QBLK
    q_blocks = q.reshape(B, n_qb, SEL_QBLK, G, HG, dh).transpose(1, 0, 2, 3, 4, 5)
    idx_blocks = sel_idx.reshape(B, G, n_qb, SEL_QBLK, n_sel).transpose(2, 0, 1, 3, 4)
    qpos_blocks = qpos.reshape(n_qb, SEL_QBLK)
    b_idx = jnp.arange(B)[:, None, None, None]
    g_idx = jnp.arange(G)[None, :, None, None]
    n_keys = n_sel * SEL_BLOCK

    def sel_block(args):
        qb, ib, pb = args
        kg = ks_blk[b_idx, g_idx, ib].reshape(B, G, SEL_QBLK, n_keys, dh)
        vg = vs_blk[b_idx, g_idx, ib].reshape(B, G, SEL_QBLK, n_keys, dh)
        kpos = (ib[..., None] * SEL_BLOCK + jnp.arange(SEL_BLOCK)).reshape(B, G, SEL_QBLK, n_keys)
        dist = pb[None, None, :, None] - kpos
        bias = jnp.moveaxis(bias_g[rel_bucket(dist), g_idx], -1, 2)
        s = jnp.einsum('bqgjd,bgqkd->bgjqk', qb, kg) + bias
        p = masked_softmax(s, (dist >= 0)[:, :, None])
        return jnp.einsum('bgjqk,bgqkd->bqgjd', p.astype(vg.dtype), vg)

    o_s = lax.map(sel_block, (q_blocks, idx_blocks, qpos_blocks))
    o_s = o_s.transpose(1, 0, 2, 3, 4, 5).reshape(B, S, G, HG, dh)

    n_wb = S // WIN_QBLK
    wb = WINDOW // WIN_QBLK

    def band(t):
        tp = jnp.pad(t, ((0, 0), (WINDOW, 0), (0, 0), (0, 0))).reshape(B, n_wb + wb, WIN_QBLK, G, dh)
        return jnp.concatenate([tp[:, o:o + n_wb] for o in range(wb + 1)], axis=2)

    kw, vw = band(k_w), band(v_w)
    qw = q.reshape(B, n_wb, WIN_QBLK, G, HG, dh)
    qp = qpos.reshape(n_wb, WIN_QBLK)
    kp = (jnp.arange(n_wb)[:, None] - wb) * WIN_QBLK + jnp.arange((wb + 1) * WIN_QBLK)[None, :]
    dist_w = qp[:, :, None] - kp[:, None, :]
    valid_w = (dist_w >= 0) & (dist_w < WINDOW) & (kp[:, None, :] >= 0)
    bias_w = jnp.moveaxis(bias_g[rel_bucket(dist_w)], (3, 4), (0, 1))
    s_w = jnp.einsum('bnqgjd,bnkgd->bgjnqk', qw, kw) + bias_w
    p_w = masked_softmax(s_w, valid_w)
    o_w = jnp.einsum('bgjnqk,bnkgd->bnqgjd', p_w.astype(vw.dtype), vw).reshape(B, S, G, HG, dh)

    o = gates[..., 0:1] * o_c + gates[..., 1:2] * o_s + gates[..., 2:3] * o_w
    return o.reshape(B, S, NSA_Q)


def rwkv7(z, mu, vecs, w2, a2, g2, r_k, ln):
    B, S = z.shape[0], z.shape[1]
    H, N = RWKV_HEADS, RWKV_HD
    prev = jnp.pad(z, ((0, 0), (1, 0), (0, 0)))[:, :-1]
    z = z + (prev - z) * mu
    r, k, v, xw, xa, xg = split_last(z, (RWKV_DIM, RWKV_DIM, RWKV_DIM, DECAY_LORA, AAA_LORA, GATE_LORA))
    w0, a0, k_k, k_a = vecs[0], vecs[1], vecs[2], vecs[3]
    w_log = -jax.nn.softplus(-(w0 + jnp.tanh(xw) @ w2)) - 0.5
    decay = jnp.exp(-jnp.exp(w_log.astype(jnp.float32)))
    a = jax.nn.sigmoid(a0 + xa @ a2)
    g = jax.nn.sigmoid(xg) @ g2

    def heads(t):
        return t.reshape(B, S, H, N).astype(jnp.float32)

    kk = heads(k * k_k)
    kk = kk / jnp.maximum(jnp.linalg.norm(kk, axis=-1, keepdims=True), 1e-12)
    k = k * (1 + (a - 1) * k_a)
    r_h, k_h, v_h, a_h, w_h = heads(r), heads(k), heads(v), heads(a), heads(decay)

    def step(state, inp):
        r_t, w_t, k_t, v_t, kk_t, a_t = inp
        sa = jnp.einsum('bhij,bhj->bhi', state, kk_t)
        state = (state * w_t[:, :, None, :] - sa[..., :, None] * (kk_t * a_t)[:, :, None, :]
                 + v_t[..., :, None] * k_t[:, :, None, :])
        return state, jnp.einsum('bhij,bhj->bhi', state, r_t)

    xs = tuple(jnp.moveaxis(t, 1, 0) for t in (r_h, w_h, k_h, v_h, kk, a_h))
    _, y = lax.scan(step, jnp.zeros((B, H, N, N), jnp.float32), xs)
    y = jnp.moveaxis(y, 0, 1)
    y = head_layer_norm(y, RWKV_LN_EPS).reshape(B, S, RWKV_DIM) * ln[0] + ln[1]
    bonus = jnp.sum(r_h * k_h * r_k, axis=-1, keepdims=True) * v_h
    y = y + bonus.reshape(B, S, RWKV_DIM)
    return (y * g).astype(z.dtype)


def setup_inputs(seed: int = 0) -> dict:
    key = jax.random.key(seed)
    ks = iter(jax.random.split(key, 48))
    L, D = DEPTH, D_MODEL

    def nrm(shape, scale):
        return scale * jax.random.normal(next(ks), shape, jnp.float32)

    def unif(shape, lo, hi):
        return jax.random.uniform(next(ks), shape, jnp.float32, lo, hi)

    rwkv_vecs = jnp.stack([unif((L, RWKV_DIM), -6.0, -1.0),
                           nrm((L, RWKV_DIM), 0.1),
                           0.85 + nrm((L, RWKV_DIM), 0.05),
                           1.0 + nrm((L, RWKV_DIM), 0.05)], axis=1)
    rwkv_ln = jnp.stack([1.0 + nrm((L, RWKV_DIM), 0.02), nrm((L, RWKV_DIM), 0.02)], axis=1)
    return {
        "x": nrm((BATCH, SEQ, D), 1.0),
        "c": nrm((BATCH, D), 1.0),
        "rel_bias": nrm((REL_BUCKETS, NSA_HEADS), 0.5),
        "w_in": nrm((L, D, D_IN), D ** -0.5),
        "w_branch_ret": nrm((L, RET_V, D), RET_V ** -0.5),
        "w_branch_nsa": nrm((L, NSA_Q, D), NSA_Q ** -0.5),
        "w_branch_rwkv": nrm((L, RWKV_DIM, D), RWKV_DIM ** -0.5),
        "w_out": nrm((L, D, D), D ** -0.5),
        "ffn1_in": nrm((L, D, 2 * D_FF), D ** -0.5),
        "ffn1_out": nrm((L, D_FF, D), D_FF ** -0.5),
        "ffn2_in": nrm((L, D, 2 * D_FF), D ** -0.5),
        "ffn2_out": nrm((L, D_FF, D), D_FF ** -0.5),
        "ada_down": nrm((L, D, ADA_RANK), D ** -0.5),
        "ada_up": nrm((L, ADA_RANK, 3 * N_SUB * D), 0.5 * ADA_RANK ** -0.5),
        "ada_bias": nrm((L, 3 * N_SUB * D), 0.02),
        "norm_pre": 1.0 + nrm((L, N_SUB, D), 0.02),
        "norm_post": 1.0 + nrm((L, N_SUB, D), 0.02),
        "cmp_pos": nrm((L, 2, CMP_BLOCK, NSA_HD), 0.1),
        "cmp_w1": nrm((L, 2, CMP_BLOCK * NSA_HD, CMP_HIDDEN), (CMP_BLOCK * NSA_HD) ** -0.5),
        "cmp_b1": nrm((L, 2, CMP_HIDDEN), 0.02),
        "cmp_w2": nrm((L, 2, CMP_HIDDEN, NSA_HD), CMP_HIDDEN ** -0.5),
        "cmp_b2": nrm((L, 2, NSA_HD), 0.02),
        "rwkv_mu": unif((L, RWKV_COLS), 0.0, 1.0),
        "rwkv_vecs": rwkv_vecs,
        "rwkv_w2": nrm((L, DECAY_LORA, RWKV_DIM), DECAY_LORA ** -0.5),
        "rwkv_a2": nrm((L, AAA_LORA, RWKV_DIM), AAA_LORA ** -0.5),
        "rwkv_g2": nrm((L, GATE_LORA, RWKV_DIM), GATE_LORA ** -0.5),
        "rwkv_rk": nrm((L, RWKV_HEADS, RWKV_HD), 0.1),
        "rwkv_ln": rwkv_ln,
    }


def reference(x, c, rel_bias, w_in, w_branch_ret, w_branch_nsa, w_branch_rwkv, w_out,
              ffn1_in, ffn1_out, ffn2_in, ffn2_out, ada_down, ada_up, ada_bias,
              norm_pre, norm_post, cmp_pos, cmp_w1, cmp_b1, cmp_w2, cmp_b2,
              rwkv_mu, rwkv_vecs, rwkv_w2, rwkv_a2, rwkv_g2, rwkv_rk, rwkv_ln):
    B, S, D = x.shape
    for l in range(DEPTH):
        mod = (jax.nn.silu(c) @ ada_down[l] @ ada_up[l] + ada_bias[l]).reshape(B, N_SUB, 3, D)
        shift, scale, gate = mod[:, :, 0], mod[:, :, 1], mod[:, :, 2]
        h = modulate(x, norm_pre[l, 0], shift[:, 0], scale[:, 0])
        x = x + 0.5 * gated_post(swiglu(h, ffn1_in[l], ffn1_out[l]), norm_post[l, 0], gate[:, 0])
        h = modulate(x, norm_pre[l, 1], shift[:, 1], scale[:, 1])
        z_gate, z_ret, z_nsa, z_rwkv = split_last(h @ w_in[l], IN_GROUPS)
        g_ret, g_nsa, g_rwkv = jnp.split(jax.nn.sigmoid(z_gate), 3, axis=-1)
        y_ret = retention(z_ret) @ w_branch_ret[l]
        y_nsa = nsa(z_nsa, rel_bias, cmp_pos[l], cmp_w1[l], cmp_b1[l], cmp_w2[l], cmp_b2[l]) @ w_branch_nsa[l]
        y_rwkv = rwkv7(z_rwkv, rwkv_mu[l], rwkv_vecs[l], rwkv_w2[l], rwkv_a2[l], rwkv_g2[l],
                       rwkv_rk[l], rwkv_ln[l]) @ w_branch_rwkv[l]
        merged = g_ret * y_ret + g_nsa * y_nsa + g_rwkv * y_rwkv
        x = x + gated_post(merged @ w_out[l], norm_post[l, 1], gate[:, 1])
        h = modulate(x, norm_pre[l, 2], shift[:, 2], scale[:, 2])
        x = x + 0.5 * gated_post(swiglu(h, ffn2_in[l], ffn2_out[l]), norm_post[l, 2], gate[:, 2])
    return x
```

```python
import functools
import math

import numpy as np
import jax
import jax.numpy as jnp
from jax import lax
from jax.experimental import pallas as pl
from jax.experimental.pallas import tpu as pltpu

F32 = jnp.float32
BF16 = jnp.bfloat16

D_MODEL = 4096
DEPTH = 4
D_FF = 3072
N_SUB = 3
RET_HEADS, RET_QK_HD, RET_V_HD, RET_CHUNK = 8, 128, 256, 128
RET_QK = RET_HEADS * RET_QK_HD
RET_V = RET_HEADS * RET_V_HD
ROPE_BASE = 10000.0
NSA_HEADS, NSA_KV_HEADS, NSA_HD = 8, 2, 128
NSA_GROUP = NSA_HEADS // NSA_KV_HEADS
NSA_Q = NSA_HEADS * NSA_HD
NSA_KV = NSA_KV_HEADS * NSA_HD
CMP_BLOCK, CMP_STRIDE = 32, 16
SEL_BLOCK, N_SEL = 64, 8
WINDOW = 512
RWKV_HD, RWKV_DIM = 64, 1024
RWKV_HEADS = RWKV_DIM // RWKV_HD
DECAY_LORA, AAA_LORA, GATE_LORA = 64, 64, 160
RWKV_LN_EPS = 64e-5
REL_BUCKETS, REL_MAX_DIST = 32, 128
RET_COLS = 2 * RET_QK + 2 * RET_V
NSA_COLS = NSA_Q + 6 * NSA_KV + 3 * NSA_HEADS
RWKV_COLS = 3 * RWKV_DIM + DECAY_LORA + AAA_LORA + GATE_LORA
NEG = -1e30

LANES = 128
VMEM_LIMIT = 56 * 1024 * 1024

ZC_RET = 0
ZC_RWKV = RET_COLS
ZC_NSA_Q = ZC_RWKV + 3 * RWKV_DIM
ZC_NSA_KV = ZC_NSA_Q + NSA_Q
ZC_NSA_GATE = ZC_NSA_KV + 6 * NSA_KV
NSA_GATE_PAD = LANES
ZC_LORA = ZC_NSA_GATE + NSA_GATE_PAD
LORA_COLS = DECAY_LORA + AAA_LORA + GATE_LORA
LORA_PAD = 3 * LANES
Z_COLS = ZC_LORA + LORA_PAD

RWKV_CHUNK = 64
RWKV_GROUP = 4
RWKV_GW = RWKV_GROUP * RWKV_HD


def _cparams(sem):
    return pltpu.CompilerParams(dimension_semantics=sem, vmem_limit_bytes=VMEM_LIMIT)


def _dot(a, b):
    return jnp.dot(a.astype(BF16), b.astype(BF16), preferred_element_type=F32)


def _dot_nt(a, b):
    return lax.dot_general(a.astype(BF16), b.astype(BF16), (((1,), (1,)), ((), ())),
                           preferred_element_type=F32)


def _dot_tn(a, b):
    return lax.dot_general(a.astype(BF16), b.astype(BF16), (((0,), (0,)), ((), ())),
                           preferred_element_type=F32)


def _split_dot(x, w_bf16, parts):
    acc = None
    rem = x
    for _ in range(parts):
        hi = rem.astype(BF16)
        t = jnp.dot(hi, w_bf16, preferred_element_type=F32)
        acc = t if acc is None else acc + t
        rem = rem - hi.astype(F32)
    return acc


def _silu(x):
    return x * jax.nn.sigmoid(x)


def _ada_kernel(c_ref, down_ref, up_ref, bias_ref, o_ref):
    t = _dot(_silu(c_ref[...]), down_ref[...])
    o_ref[...] = _dot(t, up_ref[...]) + bias_ref[...]


def ada_mod(c, ada_down, ada_up, ada_bias):
    L, D, R = ada_down.shape
    B = c.shape[0]
    N = ada_up.shape[-1]
    tn = D
    out = pl.pallas_call(
        _ada_kernel,
        out_shape=jax.ShapeDtypeStruct((L, B, N), F32),
        grid=(L, N // tn),
        in_specs=[
            pl.BlockSpec((B, D), lambda l, j: (0, 0)),
            pl.BlockSpec((None, D, R), lambda l, j: (l, 0, 0)),
            pl.BlockSpec((None, R, tn), lambda l, j: (l, 0, j)),
            pl.BlockSpec((None, 1, tn), lambda l, j: (l, 0, j)),
        ],
        out_specs=pl.BlockSpec((None, B, tn), lambda l, j: (l, 0, j)),
        compiler_params=_cparams(("parallel", "parallel")),
        name="ada_mod",
    )(c, ada_down, ada_up, ada_bias.reshape(L, 1, N))
    return out.reshape(L, B, 3 * N_SUB, D)


def _rms(x, g):
    return x * lax.rsqrt(jnp.mean(x * x, axis=-1, keepdims=True) + 1e-6) * g


def _normmod_kernel(x_ref, g_ref, mod_ref, h_ref, *, sub):
    shift = mod_ref[3 * sub:3 * sub + 1, :]
    scale = mod_ref[3 * sub + 1:3 * sub + 2, :]
    h_ref[0] = (_rms(x_ref[0], g_ref[...]) * (1.0 + scale) + shift).astype(h_ref.dtype)


def norm_modulate(x, gain, mod, l, sub, ts):
    B, S, D = x.shape
    return pl.pallas_call(
        functools.partial(_normmod_kernel, sub=sub),
        out_shape=jax.ShapeDtypeStruct((B, S, D), BF16),
        grid=(B, S // ts),
        in_specs=[
            pl.BlockSpec((1, ts, D), lambda b, i: (b, i, 0)),
            pl.BlockSpec((None, None, 1, D), lambda b, i: (l, sub, 0, 0)),
            pl.BlockSpec((None, None, 3 * N_SUB, D), lambda b, i: (l, b, 0, 0)),
        ],
        out_specs=pl.BlockSpec((1, ts, D), lambda b, i: (b, i, 0)),
        compiler_params=_cparams(("parallel", "parallel")),
        name="norm_modulate",
    )(x, gain, mod)


def _post_kernel(x_ref, y_ref, gpost_ref, mod_ref, gpre_ref, modn_ref, xo_ref, h_ref, *, sub, coef, nsub):
    gate = mod_ref[3 * sub + 2:3 * sub + 3, :]
    xn = x_ref[0] + coef * (gate * _rms(y_ref[0], gpost_ref[...]))
    xo_ref[0] = xn
    shift = modn_ref[3 * nsub:3 * nsub + 1, :]
    scale = modn_ref[3 * nsub + 1:3 * nsub + 2, :]
    h_ref[0] = (_rms(xn, gpre_ref[...]) * (1.0 + scale) + shift).astype(h_ref.dtype)


def _post_last_kernel(x_ref, y_ref, gpost_ref, mod_ref, xo_ref, *, sub, coef):
    gate = mod_ref[3 * sub + 2:3 * sub + 3, :]
    xo_ref[0] = x_ref[0] + coef * (gate * _rms(y_ref[0], gpost_ref[...]))


def post_residual(x, y, norm_post, norm_pre, mod, l, sub, coef, nxt, ts):
    B, S, D = x.shape
    xspec = pl.BlockSpec((1, ts, D), lambda b, i: (b, i, 0))
    gspec = lambda ll, ss: pl.BlockSpec((None, None, 1, D), lambda b, i: (ll, ss, 0, 0))
    mspec = lambda ll: pl.BlockSpec((None, None, 3 * N_SUB, D), lambda b, i: (ll, b, 0, 0))
    if nxt is None:
        return pl.pallas_call(
            functools.partial(_post_last_kernel, sub=sub, coef=coef),
            out_shape=jax.ShapeDtypeStruct((B, S, D), F32),
            grid=(B, S // ts),
            in_specs=[xspec, xspec, gspec(l, sub), mspec(l)],
            out_specs=xspec,
            compiler_params=_cparams(("parallel", "parallel")),
            name="post_last",
        )(x, y, norm_post, mod), None
    l2, sub2 = nxt
    return pl.pallas_call(
        functools.partial(_post_kernel, sub=sub, coef=coef, nsub=sub2),
        out_shape=(jax.ShapeDtypeStruct((B, S, D), F32), jax.ShapeDtypeStruct((B, S, D), BF16)),
        grid=(B, S // ts),
        in_specs=[xspec, xspec, gspec(l, sub), mspec(l), gspec(l2, sub2), mspec(l2)],
        out_specs=(xspec, xspec),
        compiler_params=_cparams(("parallel", "parallel")),
        name="post_residual",
    )(x, y, norm_post, mod, norm_pre, mod)


def _mm_kernel(a_ref, w_ref, o_ref, *, act):
    acc = jnp.dot(a_ref[...], w_ref[...], preferred_element_type=F32)
    if act == "sigmoid":
        acc = jax.nn.sigmoid(acc)
    o_ref[...] = acc.astype(o_ref.dtype)


def matmul(a, w, l, tm, tn, out_dtype, act=None, name="matmul"):
    M, K = a.shape
    N = w.shape[-1]
    return pl.pallas_call(
        functools.partial(_mm_kernel, act=act),
        out_shape=jax.ShapeDtypeStruct((M, N), out_dtype),
        grid=(M // tm, N // tn),
        in_specs=[
            pl.BlockSpec((tm, K), lambda i, j: (i, 0)),
            pl.BlockSpec((None, K, tn), lambda i, j: (l, 0, j)),
        ],
        out_specs=pl.BlockSpec((tm, tn), lambda i, j: (i, j)),
        compiler_params=_cparams(("parallel", "arbitrary")),
        name=name,
    )(a, w)


def _swiglu_kernel(a_ref, wa_ref, wb_ref, o_ref):
    h = a_ref[...]
    a = jnp.dot(h, wa_ref[...], preferred_element_type=F32)
    b = jnp.dot(h, wb_ref[...], preferred_element_type=F32)
    o_ref[...] = (_silu(a) * b).astype(o_ref.dtype)


def swiglu_in(h, w, l, tm, tn):
    M, K = h.shape
    F = w.shape[-1] // 2
    nb = F // tn
    return pl.pallas_call(
        _swiglu_kernel,
        out_shape=jax.ShapeDtypeStruct((M, F), BF16),
        grid=(M // tm, nb),
        in_specs=[
            pl.BlockSpec((tm, K), lambda i, j: (i, 0)),
            pl.BlockSpec((None, K, tn), lambda i, j: (l, 0, j)),
            pl.BlockSpec((None, K, tn), lambda i, j: (l, 0, j + nb)),
        ],
        out_specs=pl.BlockSpec((tm, tn), lambda i, j: (i, j)),
        compiler_params=_cparams(("parallel", "arbitrary")),
        name="swiglu_in",
    )(h, w, w)


def _merge_kernel(o1_ref, o2_ref, o3_ref, w1_ref, w2_ref, w3_ref, g1_ref, g2_ref, g3_ref, o_ref):
    acc = g1_ref[...].astype(F32) * jnp.dot(o1_ref[...], w1_ref[...], preferred_element_type=F32)
    acc += g2_ref[...].astype(F32) * jnp.dot(o2_ref[...], w2_ref[...], preferred_element_type=F32)
    acc += g3_ref[...].astype(F32) * jnp.dot(o3_ref[...], w3_ref[...], preferred_element_type=F32)
    o_ref[...] = acc.astype(o_ref.dtype)


def branch_merge(o_ret, o_nsa, o_rwkv, w_ret, w_nsa, w_rwkv, zg, l, tm, tn):
    M = o_ret.shape[0]
    D = w_ret.shape[-1]
    nb = D // tn
    ospec = lambda o: pl.BlockSpec((tm, o.shape[1]), lambda i, j: (i, 0))
    wspec = lambda w: pl.BlockSpec((None, w.shape[1], tn), lambda i, j: (l, 0, j))
    gspec = lambda k: pl.BlockSpec((tm, tn), lambda i, j: (i, j + k * nb))
    return pl.pallas_call(
        _merge_kernel,
        out_shape=jax.ShapeDtypeStruct((M, D), BF16),
        grid=(M // tm, nb),
        in_specs=[ospec(o_ret), ospec(o_nsa), ospec(o_rwkv), wspec(w_ret), wspec(w_nsa), wspec(w_rwkv),
                  gspec(0), gspec(1), gspec(2)],
        out_specs=pl.BlockSpec((tm, tn), lambda i, j: (i, j)),
        compiler_params=_cparams(("parallel", "arbitrary")),
        name="branch_merge",
    )(o_ret, o_nsa, o_rwkv, w_ret, w_nsa, w_rwkv, zg, zg, zg)


def _ret_kernel(q_ref, k_ref, v_ref, g_ref, cos_ref, sin_ref, dm_ref, zeta_ref, xi_ref, o_ref, r_scr, *, decays):
    @pl.when(pl.program_id(1) == 0)
    def _():
        r_scr[...] = jnp.zeros_like(r_scr)

    cos = cos_ref[...]
    sin = sin_ref[...]
    dk, dv = RET_QK_HD, RET_V_HD
    for h in range(RET_HEADS):
        qh = q_ref[0, :, h * dk:(h + 1) * dk]
        kh = k_ref[0, :, h * dk:(h + 1) * dk]
        qh = qh * cos + pltpu.roll(qh, dk // 2, 1) * sin
        kh = (kh * cos + pltpu.roll(kh, dk // 2, 1) * sin) * (dk ** -0.5)
        vh = v_ref[0, :, h * dv:(h + 1) * dv]
        s = _dot_nt(qh, kh) * dm_ref[h]
        state = r_scr[h]
        o = _dot(s, vh) + _dot(qh, state) * xi_ref[h]
        mu = jnp.mean(o, axis=-1, keepdims=True)
        d = o - mu
        var = jnp.mean(d * d, axis=-1, keepdims=True)
        on = d * lax.rsqrt(var + 1e-6)
        gh = g_ref[0, :, h * dv:(h + 1) * dv]
        o_ref[0, :, h * dv:(h + 1) * dv] = (_silu(gh) * on).astype(o_ref.dtype)
        r_scr[h] = _dot_tn(kh * zeta_ref[h], vh) + decays[h] * state


def retention(z3):
    B, S, _ = z3.shape
    H, C = RET_HEADS, RET_CHUNK
    pos = jnp.arange(S, dtype=F32)
    inv = 1.0 / (ROPE_BASE ** jnp.linspace(0.0, 1.0, RET_QK_HD // 2))
    ang = pos[:, None] * inv[None, :]
    cos, sin = jnp.cos(ang), jnp.sin(ang)
    cos_f = jnp.concatenate([cos, cos], axis=-1)
    sin_f = jnp.concatenate([-sin, sin], axis=-1)
    log_g = jnp.log1p(-(2.0 ** (-5.0 - jnp.arange(H, dtype=F32))))
    idx = jnp.arange(C, dtype=F32)
    diff = idx[:, None] - idx[None, :]
    dmask = jnp.where(diff >= 0, jnp.exp(jnp.maximum(diff, 0.0)[None] * log_g[:, None, None]), 0.0)
    zeta = jnp.exp((C - 1 - idx)[None, :] * log_g[:, None])
    xi = jnp.exp((idx + 1)[None, :] * log_g[:, None])
    zeta_t = jnp.broadcast_to(zeta[:, :, None], (H, C, RET_QK_HD))
    xi_t = jnp.broadcast_to(xi[:, :, None], (H, C, RET_V_HD))
    lg64 = np.log1p(-(2.0 ** (-5.0 - np.arange(H, dtype=np.float64))))
    decays = tuple(float(v) for v in np.exp(C * lg64))
    qb = ZC_RET // RET_QK
    const = lambda shape: pl.BlockSpec(shape, lambda b, c: (0,) * len(shape))
    return pl.pallas_call(
        functools.partial(_ret_kernel, decays=decays),
        out_shape=jax.ShapeDtypeStruct((B, S, RET_V), BF16),
        grid=(B, S // C),
        in_specs=[
            pl.BlockSpec((1, C, RET_QK), lambda b, c: (b, c, qb)),
            pl.BlockSpec((1, C, RET_QK), lambda b, c: (b, c, qb + 1)),
            pl.BlockSpec((1, C, RET_V), lambda b, c: (b, c, (ZC_RET + 2 * RET_QK) // RET_V)),
            pl.BlockSpec((1, C, RET_V), lambda b, c: (b, c, (ZC_RET + 2 * RET_QK) // RET_V + 1)),
            pl.BlockSpec((C, RET_QK_HD), lambda b, c: (c, 0)),
            pl.BlockSpec((C, RET_QK_HD), lambda b, c: (c, 0)),
            const((H, C, C)), const((H, C, RET_QK_HD)), const((H, C, RET_V_HD)),
        ],
        out_specs=pl.BlockSpec((1, C, RET_V), lambda b, c: (b, c, 0)),
        scratch_shapes=[pltpu.VMEM((H, RET_QK_HD, RET_V_HD), F32)],
        compiler_params=_cparams(("parallel", "arbitrary")),
        name="retention",
    )(z3, z3, z3, z3, cos_f, sin_f, dmask, zeta_t, xi_t)


def _rel_bucket(dist):
    n = jnp.maximum(dist, 0)
    max_exact = REL_BUCKETS // 2
    nf = jnp.maximum(n, 1).astype(F32)
    large = max_exact + (jnp.log(nf / max_exact) / math.log(REL_MAX_DIST / max_exact)
                         * (REL_BUCKETS - max_exact)).astype(jnp.int32)
    large = jnp.minimum(large, REL_BUCKETS - 1)
    return jnp.where(n < max_exact, n, large)


def _cmp_kernel(kc_ref, vc_ref, pos_ref, w1_ref, b1_ref, w2_ref, b2_ref, ko_ref, vo_ref, *, nb):
    d = NSA_HD
    half = CMP_BLOCK // 2
    for i, (src, dst) in enumerate(((kc_ref, ko_ref), (vc_ref, vo_ref))):
        p1 = jnp.zeros((nb, w1_ref.shape[-1]), F32)
        p2 = jnp.zeros((nb, w1_ref.shape[-1]), F32)
        for t in range(half):
            a = src[0, pl.ds(t, nb, stride=CMP_STRIDE), :]
            p1 += _dot(a + pos_ref[i, t:t + 1, :], w1_ref[i, t * d:(t + 1) * d, :])
            p2 += _dot(a + pos_ref[i, half + t:half + t + 1, :], w1_ref[i, (half + t) * d:(half + t + 1) * d, :])
        pre = p1 + pltpu.roll(p2, nb - 1, 0) + b1_ref[i]
        dst[0, 0] = _dot(jax.nn.gelu(pre), w2_ref[i]) + b2_ref[i]


def nsa_compress(z3, cmp_pos, cmp_w1, cmp_b1, cmp_w2, cmp_b2):
    B, S, _ = z3.shape
    G, d = NSA_KV_HEADS, NSA_HD
    nb = S // CMP_STRIDE
    kb = ZC_NSA_KV // d
    hid = cmp_w1.shape[-1]
    const = lambda shape: pl.BlockSpec(shape, lambda b, g: (0,) * len(shape))
    out = jax.ShapeDtypeStruct((B, G, nb, d), F32)
    return pl.pallas_call(
        functools.partial(_cmp_kernel, nb=nb),
        out_shape=(out, out),
        grid=(B, G),
        in_specs=[
            pl.BlockSpec((1, S, d), lambda b, g: (b, 0, kb + g)),
            pl.BlockSpec((1, S, d), lambda b, g: (b, 0, kb + G + g)),
            const((2, CMP_BLOCK, d)), const((2, CMP_BLOCK * d, hid)), const((2, 1, hid)),
            const((2, hid, d)), const((2, 1, d)),
        ],
        out_specs=(pl.BlockSpec((1, 1, nb, d), lambda b, g: (b, g, 0, 0)),
                   pl.BlockSpec((1, 1, nb, d), lambda b, g: (b, g, 0, 0))),
        compiler_params=_cparams(("parallel", "parallel")),
        name="nsa_compress",
    )(z3, z3, cmp_pos, cmp_w1, cmp_b1.reshape(2, 1, hid), cmp_w2, cmp_b2.reshape(2, 1, d))


def _nsa_kernel(q_ref, gate_ref, kc_ref, vc_ref, ks_ref, vs_ref, kw_ref, vw_ref, bc_ref, bt_ref, ov_ref, e_ref,
                o_ref, q_scr, msel_scr, m_scr, l_scr, acc_scr, *, n_cmp, n_sblk, nkb):
    g = pl.program_id(1)
    qi = pl.program_id(2)
    T = LANES
    HG = NSA_GROUP
    row = lax.broadcasted_iota(jnp.int32, (T, T), 0)
    col = lax.broadcasted_iota(jnp.int32, (T, T), 1)
    qpos = qi * T + row

    for j in range(HG):
        q_scr[j] = q_ref[0, :, j * T:(j + 1) * T] * (NSA_HD ** -0.5)

    kc = kc_ref[0, 0]
    vc = vc_ref[0, 0]
    valid_c = ((qpos - (CMP_STRIDE * col + CMP_BLOCK - 1)) >= 0) & (col < n_cmp)
    psum = jnp.zeros((T, T), F32)
    o_c = []
    for j in range(HG):
        s = jnp.where(valid_c, _dot_nt(q_scr[j], kc) + bc_ref[j], NEG)
        m = jnp.max(s, axis=-1, keepdims=True)
        p = jnp.where(valid_c, jnp.exp(s - m), 0.0)
        den = jnp.sum(p, axis=-1, keepdims=True)
        pn = p / jnp.where(den > 0.0, den, 1.0)
        psum += pn
        o_c.append(_dot(pn, vc))

    imp = _split_dot(psum, ov_ref[...], 2)
    cur = qpos // SEL_BLOCK
    forced = (col == 0) | (col == cur) | (col == cur - 1)
    impm = jnp.where(forced, 1e30, jnp.where(col <= cur, imp, NEG))
    rank = jnp.zeros((T, T), F32)
    for mb in range(n_sblk):
        cm = impm[:, mb:mb + 1]
        lower = jnp.where(col > mb, 1.0, 0.0)
        rank += jnp.where(cm > impm, 1.0, jnp.where(cm == impm, lower, 0.0))
    sel = jnp.where(rank < float(min(N_SEL, n_sblk)), 1.0, 0.0).astype(BF16)
    for kj in range(nkb):
        msel_scr[kj] = jnp.dot(sel, e_ref[kj], preferred_element_type=F32)

    def attend(k_ref, v_ref, lo, use_sel):
        m_scr[...] = jnp.full(m_scr.shape, NEG, F32)
        l_scr[...] = jnp.zeros_like(l_scr)
        acc_scr[...] = jnp.zeros_like(acc_scr)

        def body(kj, carry):
            start = pl.multiple_of(kj * T, T)
            kb = k_ref[0, pl.ds(start, T), :]
            vb = v_ref[0, pl.ds(start, T), :]
            dist = (qi - kj) * T + row - col
            if use_sel:
                valid = (dist >= 0) & (msel_scr[kj] > 0.5)
            else:
                valid = (dist >= 0) & (dist < WINDOW)
            bidx = jnp.minimum(qi - kj, 2)
            for j in range(HG):
                s = jnp.where(valid, _dot_nt(q_scr[j], kb) + bt_ref[j, bidx], NEG)
                m_old = m_scr[j]
                m_new = jnp.maximum(m_old, jnp.max(s, axis=-1, keepdims=True))
                p = jnp.where(valid, jnp.exp(s - m_new), 0.0)
                alpha = jnp.exp(m_old - m_new)
                l_scr[j] = alpha * l_scr[j] + jnp.sum(p, axis=-1, keepdims=True)
                acc_scr[j] = alpha * acc_scr[j] + _dot(p, vb)
                m_scr[j] = m_new
            return carry

        lax.fori_loop(lo, qi + 1, body, 0)
        return [acc_scr[j] / l_scr[j] for j in range(HG)]

    o_s = attend(ks_ref, vs_ref, 0, True)
    o_w = attend(kw_ref, vw_ref, jnp.maximum(qi - WINDOW // T, 0), False)

    gates = jax.nn.sigmoid(gate_ref[0])
    for j in range(HG):
        base = 3 * j
        gsel = lambda c: jnp.where(g == 0, gates[:, c:c + 1], gates[:, 3 * HG + c:3 * HG + c + 1])
        o = gsel(base) * o_c[j] + gsel(base + 1) * o_s[j] + gsel(base + 2) * o_w[j]
        o_ref[0, :, j * T:(j + 1) * T] = o.astype(o_ref.dtype)


def nsa_attention(z3, kcmp, vcmp, rel_bias):
    B, S, _ = z3.shape
    G, HG, d, T = NSA_KV_HEADS, NSA_GROUP, NSA_HD, LANES
    assert G == 2 and d == T
    n_cmp = (S - CMP_BLOCK) // CMP_STRIDE + 1
    n_sblk = S // SEL_BLOCK
    nkb = S // T
    assert n_cmp < T and n_sblk <= T and S // CMP_STRIDE == T

    far = np.arange(T + 1, S + T, dtype=np.float64)
    half = REL_BUCKETS // 2
    bfar = half + np.floor(np.log(far / half) / math.log(REL_MAX_DIST / half) * (REL_BUCKETS - half) - 1e-3)
    assert (bfar >= REL_BUCKETS - 1).all(), "bias bucket must be constant beyond one key block"
    tab = rel_bias[_rel_bucket(jnp.arange(3 * T)), :].T
    ii = jnp.arange(T)
    dloc = ii[:, None] - ii[None, :]
    tiles = jnp.stack([tab[:, jnp.maximum(dloc, 0)], tab[:, T + dloc],
                       jnp.broadcast_to(tab[:, 2 * T + 1][:, None, None], (NSA_HEADS, T, T))], axis=1)
    dist_c = jnp.arange(S)[:, None] - (CMP_STRIDE * jnp.arange(T)[None, :] + CMP_BLOCK - 1)
    bias_c = rel_bias[_rel_bucket(dist_c), :].transpose(2, 0, 1)

    cmp_start = CMP_STRIDE * np.arange(T)
    sel_start = SEL_BLOCK * np.arange(T)
    ov = ((cmp_start[:, None] <= (sel_start + SEL_BLOCK - 1)[None, :])
          & ((cmp_start + CMP_BLOCK - 1)[:, None] >= sel_start[None, :])
          & (np.arange(T)[:, None] < n_cmp) & (np.arange(T)[None, :] < n_sblk))
    ov = jnp.asarray(ov, BF16)
    keyblk = (np.arange(nkb)[:, None, None] * T + np.arange(T)[None, None, :]) // SEL_BLOCK
    expand = jnp.asarray(keyblk == np.arange(T)[None, :, None], BF16)

    qb = ZC_NSA_Q // (HG * d)
    kvb = ZC_NSA_KV // d
    kvspec = lambda off: pl.BlockSpec((1, S, d), lambda b, g, i: (b, 0, kvb + off * G + g))
    cspec = pl.BlockSpec((1, 1, T, d), lambda b, g, i: (b, g, 0, 0))
    return pl.pallas_call(
        functools.partial(_nsa_kernel, n_cmp=n_cmp, n_sblk=n_sblk, nkb=nkb),
        out_shape=jax.ShapeDtypeStruct((B, S, NSA_Q), BF16),
        grid=(B, G, nkb),
        in_specs=[
            pl.BlockSpec((1, T, HG * d), lambda b, g, i: (b, i, qb + g)),
            pl.BlockSpec((1, T, NSA_GATE_PAD), lambda b, g, i: (b, i, ZC_NSA_GATE // NSA_GATE_PAD)),
            cspec, cspec, kvspec(2), kvspec(3), kvspec(4), kvspec(5),
            pl.BlockSpec((HG, T, T), lambda b, g, i: (g, i, 0)),
            pl.BlockSpec((HG, 3, T, T), lambda b, g, i: (g, 0, 0, 0)),
            pl.BlockSpec((T, T), lambda b, g, i: (0, 0)),
            pl.BlockSpec((nkb, T, T), lambda b, g, i: (0, 0, 0)),
        ],
        out_specs=pl.BlockSpec((1, T, HG * d), lambda b, g, i: (b, i, g)),
        scratch_shapes=[pltpu.VMEM((HG, T, d), F32), pltpu.VMEM((nkb, T, T), F32),
                        pltpu.VMEM((HG, T, T), F32), pltpu.VMEM((HG, T, T), F32), pltpu.VMEM((HG, T, d), F32)],
        compiler_params=_cparams(("parallel", "parallel", "arbitrary")),
        name="nsa_attention",
    )(z3, z3, kcmp, vcmp, z3, z3, z3, z3, bias_c, tiles, ov, expand)


def _rwkv_prep_kernel(zr_ref, zk_ref, zv_ref, zl_ref, mu_ref, mul_ref, vec_ref, w2_ref, a2_ref, g2_ref,
                      r_ref, k_ref, v_ref, lw_ref, kk_ref, kka_ref, gg_ref, c_scr, cl_scr):
    first = pl.program_id(1) == 0
    tb = zr_ref.shape[1]
    row = lax.broadcasted_iota(jnp.int32, (tb, 1), 0)

    def shift(z, mu, carry_ref, slot):
        prev_last = jnp.where(first, 0.0, carry_ref[slot:slot + 1, :])
        prev = jnp.where(row == 0, prev_last, pltpu.roll(z, 1, 0))
        carry_ref[slot:slot + 1, :] = z[tb - 1:tb, :]
        return z + (prev - z) * mu

    r = shift(zr_ref[0], mu_ref[0:1, :], c_scr, 0)
    k = shift(zk_ref[0], mu_ref[1:2, :], c_scr, 1)
    v = shift(zv_ref[0], mu_ref[2:3, :], c_scr, 2)
    xl = shift(zl_ref[0], mul_ref[...], cl_scr, 0)
    w0, a0, k_k, k_a = vec_ref[0:1, :], vec_ref[1:2, :], vec_ref[2:3, :], vec_ref[3:4, :]
    u = -(w0 + _dot(jnp.tanh(xl), w2_ref[...]))
    w_log = -(jnp.maximum(u, 0.0) + jnp.log1p(jnp.exp(-jnp.abs(u)))) - 0.5
    a = jax.nn.sigmoid(a0 + _dot(xl, a2_ref[...]))
    r_ref[0] = r
    k_ref[0] = k * (1.0 + (a - 1.0) * k_a)
    v_ref[0] = v
    lw_ref[0] = -jnp.exp(w_log)
    kk = k * k_k
    kk_ref[0] = kk
    kka_ref[0] = kk * a
    gg_ref[0] = _dot(jax.nn.sigmoid(xl), g2_ref[...])


def rwkv_prep(z3, mu_rkv, mu_lora, vecs, w2p, a2p, g2p, tb):
    B, S, _ = z3.shape
    Dm = RWKV_DIM
    rb = ZC_RWKV // Dm
    zspec = lambda o: pl.BlockSpec((1, tb, Dm), lambda b, i: (b, i, rb + o))
    const = lambda shape: pl.BlockSpec(shape, lambda b, i: (0,) * len(shape))
    ospec = pl.BlockSpec((1, tb, Dm), lambda b, i: (b, i, 0))
    out = jax.ShapeDtypeStruct((B, S, Dm), F32)
    return pl.pallas_call(
        _rwkv_prep_kernel,
        out_shape=(out,) * 7,
        grid=(B, S // tb),
        in_specs=[zspec(0), zspec(1), zspec(2),
                  pl.BlockSpec((1, tb, LORA_PAD), lambda b, i: (b, i, ZC_LORA // LORA_PAD)),
                  const((3, Dm)), const((1, LORA_PAD)), const((4, Dm)),
                  const((LORA_PAD, Dm)), const((LORA_PAD, Dm)), const((LORA_PAD, Dm))],
        out_specs=(ospec,) * 7,
        scratch_shapes=[pltpu.VMEM((8, Dm), F32), pltpu.VMEM((8, LORA_PAD), F32)],
        compiler_params=_cparams(("parallel", "arbitrary")),
        name="rwkv_prep",
    )(z3, z3, z3, z3, mu_rkv, mu_lora, vecs, w2p, a2p, g2p)


def _rwkv_masks():
    T, n = RWKV_CHUNK, RWKV_GW
    idx = np.arange(n)
    h, t = idx // T, idx % T
    same = h[:, None] == h[None, :]
    tt, ss = t[:, None], t[None, :]
    levels = []
    b = 1
    while b < T:
        levels.append(same & (tt // (2 * b) == ss // (2 * b)) & (tt % (2 * b) >= b) & (ss % (2 * b) < b))
        b *= 2
    masks = [same, same & (ss < tt), same & (ss <= tt), np.eye(n, dtype=bool)] + levels
    return np.stack(masks).astype(np.float32), len(levels)


def _rwkv_chunk_kernel(r_ref, k_ref, v_ref, lw_ref, kk_ref, kka_ref, gg_ref, rk_ref, ln_ref, msk_ref, tri_ref,
                       o_ref, s_scr, *, n_levels):
    T, GW = RWKV_CHUNK, RWKV_GW

    @pl.when(pl.program_id(1) == 0)
    def _():
        s_scr[...] = jnp.zeros_like(s_scr)

    m_bd = msk_ref[0]
    ones_bd = m_bd.astype(BF16)
    m_strict = msk_ref[1]
    m_incl = msk_ref[2]
    eye = msk_ref[3]
    tri = tri_ref[...].astype(BF16)

    def to_bd(x):
        return jnp.concatenate([x] * RWKV_GROUP, axis=0) * m_bd

    def from_bd(y):
        out = y[0:T]
        for i in range(1, RWKV_GROUP):
            out = out + y[i * T:(i + 1) * T]
        return out

    def seg_sum(x):
        return _split_dot(x, ones_bd, 2)

    for gi in range(RWKV_DIM // GW):
        sl = slice(gi * GW, (gi + 1) * GW)
        r, k, v, lw = r_ref[0, :, sl], k_ref[0, :, sl], v_ref[0, :, sl], lw_ref[0, :, sl]
        kk_raw, kka_raw = kk_ref[0, :, sl], kka_ref[0, :, sl]
        inv_n = 1.0 / jnp.maximum(jnp.sqrt(seg_sum(kk_raw * kk_raw)), 1e-12)
        kk = kk_raw * inv_n
        kb = kka_raw * inv_n
        cum = _split_dot_left(tri, lw)
        p_in = jnp.exp(cum)
        p_ex = jnp.exp(cum - lw)
        p_inv = jnp.exp(-cum)
        p_end = p_in[T - 1:T, :]

        a_bd = to_bd(-kk * p_ex)
        r_bd = to_bd(r * p_in)
        b_bd = to_bd(kb * p_inv)
        k_bd = to_bd(k * p_inv)
        v_bd = to_bd(v)

        a_ab = _dot_nt(a_bd, b_bd) * m_strict
        a_ak = _dot_nt(a_bd, k_bd) * m_strict
        a_rb = _dot_nt(r_bd, b_bd) * m_incl
        a_rk = _dot_nt(r_bd, k_bd) * m_incl

        x = eye + a_ab * msk_ref[4]
        for lv in range(1, n_levels):
            x = x + _dot(_dot(x, a_ab * msk_ref[4 + lv]), x)

        state = s_scr[gi]
        u = _dot(x, _dot_nt(a_bd, state) + _dot(a_ak, v_bd))
        y = from_bd(_dot_nt(r_bd, state) + _dot(a_rb, u) + _dot(a_rk, v_bd))
        s_scr[gi] = (state + _dot_tn(u, b_bd) + _dot_tn(v_bd, k_bd)) * p_end

        mu = seg_sum(y) * (1.0 / RWKV_HD)
        d = y - mu
        var = seg_sum(d * d) * (1.0 / RWKV_HD)
        yn = d * lax.rsqrt(var + RWKV_LN_EPS) * ln_ref[0:1, sl] + ln_ref[1:2, sl]
        bonus = seg_sum(r * k * rk_ref[:, sl]) * v
        o_ref[0, :, sl] = ((yn + bonus) * gg_ref[0, :, sl]).astype(o_ref.dtype)


def _split_dot_left(w_bf16, x):
    acc = None
    rem = x
    for _ in range(3):
        hi = rem.astype(BF16)
        t = jnp.dot(w_bf16, hi, preferred_element_type=F32)
        acc = t if acc is None else acc + t
        rem = rem - hi.astype(F32)
    return acc


def rwkv_chunk(r, k, v, lw, kk, kka, gg, rk, ln):
    B, S, Dm = r.shape
    T = RWKV_CHUNK
    masks, n_levels = _rwkv_masks()
    tri = np.tril(np.ones((T, T), np.float32))
    ng = Dm // RWKV_GW
    xspec = pl.BlockSpec((1, T, Dm), lambda b, c: (b, c, 0))
    const = lambda shape: pl.BlockSpec(shape, lambda b, c: (0,) * len(shape))
    return pl.pallas_call(
        functools.partial(_rwkv_chunk_kernel, n_levels=n_levels),
        out_shape=jax.ShapeDtypeStruct((B, S, Dm), BF16),
        grid=(B, S // T),
        in_specs=[xspec] * 7 + [const((1, Dm)), const((2, Dm)), const(masks.shape), const((T, T))],
        out_specs=xspec,
        scratch_shapes=[pltpu.VMEM((ng, RWKV_GW, RWKV_GW), F32)],
        compiler_params=_cparams(("parallel", "arbitrary")),
        name="rwkv_chunk",
    )(r, k, v, lw, kk, kka, gg, rk, ln, jnp.asarray(masks), jnp.asarray(tri))


def _prep_weights(w_in, rwkv_mu, rwkv_w2, rwkv_a2, rwkv_g2):
    L, D, _ = w_in.shape
    o = 3 * D
    w_gate = w_in[:, :, :o].astype(BF16)
    ret = w_in[:, :, o:o + RET_COLS]
    o += RET_COLS
    nsa_qkv = w_in[:, :, o:o + NSA_Q + 6 * NSA_KV]
    nsa_gate = w_in[:, :, o + NSA_Q + 6 * NSA_KV:o + NSA_COLS]
    o += NSA_COLS
    rkv = w_in[:, :, o:o + 3 * RWKV_DIM]
    lora = w_in[:, :, o + 3 * RWKV_DIM:o + RWKV_COLS]
    pad = lambda w, n: jnp.pad(w, ((0, 0), (0, 0), (0, n - w.shape[-1])))
    w_mix = jnp.concatenate([ret, rkv, nsa_qkv, pad(nsa_gate, NSA_GATE_PAD), pad(lora, LORA_PAD)],
                            axis=-1).astype(BF16)
    mu_rkv = rwkv_mu[:, :3 * RWKV_DIM].reshape(L, 3, RWKV_DIM)
    mu_lora = jnp.pad(rwkv_mu[:, 3 * RWKV_DIM:], ((0, 0), (0, LORA_PAD - LORA_COLS))).reshape(L, 1, LORA_PAD)
    rows = lambda w, start: jnp.pad(w, ((0, 0), (start, LORA_PAD - start - w.shape[1]), (0, 0)))
    w2p = rows(rwkv_w2, 0)
    a2p = rows(rwkv_a2, DECAY_LORA)
    g2p = rows(rwkv_g2, DECAY_LORA + AAA_LORA)
    return w_gate, w_mix, mu_rkv, mu_lora, w2p, a2p, g2p


def kernel(x, c, rel_bias, w_in, w_branch_ret, w_branch_nsa, w_branch_rwkv, w_out, ffn1_in, ffn1_out, ffn2_in,
           ffn2_out, ada_down, ada_up, ada_bias, norm_pre, norm_post, cmp_pos, cmp_w1, cmp_b1, cmp_w2, cmp_b2,
           rwkv_mu, rwkv_vecs, rwkv_w2, rwkv_a2, rwkv_g2, rwkv_rk, rwkv_ln):
    B, S, D = x.shape
    L = w_in.shape[0]
    M = B * S
    tm = min(1024, M)
    ts = min(256, S)
    tb = min(256, S)

    w_gate, w_mix, mu_rkv, mu_lora, w2p, a2p, g2p = _prep_weights(w_in, rwkv_mu, rwkv_w2, rwkv_a2, rwkv_g2)
    w_bret, w_bnsa, w_brwkv = w_branch_ret.astype(BF16), w_branch_nsa.astype(BF16), w_branch_rwkv.astype(BF16)
    w_o = w_out.astype(BF16)
    f1i, f1o, f2i, f2o = (w.astype(BF16) for w in (ffn1_in, ffn1_out, ffn2_in, ffn2_out))
    npre = norm_pre.reshape(L, N_SUB, 1, D)
    npost = norm_post.reshape(L, N_SUB, 1, D)
    rk = rwkv_rk.reshape(L, 1, RWKV_DIM)

    mod = ada_mod(c, ada_down, ada_up, ada_bias)
    h = norm_modulate(x, npre, mod, 0, 0, ts)
    for l in range(L):
        u = swiglu_in(h.reshape(M, D), f1i, l, tm, 512)
        y = matmul(u, f1o, l, tm, 512, F32, name="ffn_out")
        x, h = post_residual(x, y.reshape(B, S, D), npost, npre, mod, l, 0, 0.5, (l, 1), ts)
        h2 = h.reshape(M, D)
        zg = matmul(h2, w_gate, l, tm, 512, BF16, act="sigmoid", name="gate_proj")
        z3 = matmul(h2, w_mix, l, tm, 512, F32, name="mix_proj").reshape(B, S, Z_COLS)
        o_ret = retention(z3)
        kcmp, vcmp = nsa_compress(z3, cmp_pos[l], cmp_w1[l], cmp_b1[l], cmp_w2[l], cmp_b2[l])
        o_nsa = nsa_attention(z3, kcmp, vcmp, rel_bias)
        rw = rwkv_prep(z3, mu_rkv[l], mu_lora[l], rwkv_vecs[l], w2p[l], a2p[l], g2p[l], tb)
        o_rwkv = rwkv_chunk(*rw, rk[l], rwkv_ln[l])
        merged = branch_merge(o_ret.reshape(M, RET_V), o_nsa.reshape(M, NSA_Q), o_rwkv.reshape(M, RWKV_DIM),
                              w_bret, w_bnsa, w_brwkv, zg, l, tm, 512)
        y = matmul(merged, w_o, l, tm, 512, F32, name="out_proj")
        x, h = post_residual(x, y.reshape(B, S, D), npost, npre, mod, l, 1, 1.0, (l, 2), ts)
        u = swiglu_in(h.reshape(M, D), f2i, l, tm, 512)
        y = matmul(u, f2o, l, tm, 512, F32, name="ffn_out")
        nxt = (l + 1, 0) if l + 1 < L else None
        x, h = post_residual(x, y.reshape(B, S, D), npost, npre, mod, l, 2, 0.5, nxt, ts)
    return x
```

```python
import functools
import math

import numpy as np
import jax
import jax.numpy as jnp
from jax import lax
from jax.experimental import pallas as pl
from jax.experimental.pallas import tpu as pltpu

F32 = jnp.float32
BF16 = jnp.bfloat16

D_MODEL = 4096
DEPTH = 4
D_FF = 3072
N_SUB = 3
RET_HEADS, RET_QK_HD, RET_V_HD, RET_CHUNK = 8, 128, 256, 128
RET_QK = RET_HEADS * RET_QK_HD
RET_V = RET_HEADS * RET_V_HD
ROPE_BASE = 10000.0
NSA_HEADS, NSA_KV_HEADS, NSA_HD = 8, 2, 128
NSA_GROUP = NSA_HEADS // NSA_KV_HEADS
NSA_Q = NSA_HEADS * NSA_HD
NSA_KV = NSA_KV_HEADS * NSA_HD
CMP_BLOCK, CMP_STRIDE = 32, 16
SEL_BLOCK, N_SEL = 64, 8
WINDOW = 512
RWKV_HD, RWKV_DIM = 64, 1024
RWKV_HEADS = RWKV_DIM // RWKV_HD
DECAY_LORA, AAA_LORA, GATE_LORA = 64, 64, 160
RWKV_LN_EPS = 64e-5
REL_BUCKETS, REL_MAX_DIST = 32, 128
RET_COLS = 2 * RET_QK + 2 * RET_V
NSA_COLS = NSA_Q + 6 * NSA_KV + 3 * NSA_HEADS
RWKV_COLS = 3 * RWKV_DIM + DECAY_LORA + AAA_LORA + GATE_LORA
NEG = -1e30

LANES = 128
VMEM_LIMIT = 56 * 1024 * 1024

ZM_RET = 0
ZM_NSA_Q = RET_COLS
ZM_NSA_KV = ZM_NSA_Q + NSA_Q
ZM_COLS = ZM_NSA_KV + 6 * NSA_KV
ZT_RWKV = 0
ZT_LORA = 3 * RWKV_DIM
LORA_COLS = DECAY_LORA + AAA_LORA + GATE_LORA
LORA_PAD = 3 * LANES
ZT_GATE = ZT_LORA + LORA_PAD
NSA_GATE_PAD = LANES
ZT_COLS = ZT_GATE + NSA_GATE_PAD

RWKV_CHUNK = 64
RWKV_GROUP = 4
RWKV_GW = RWKV_GROUP * RWKV_HD


def _cparams(sem):
    return pltpu.CompilerParams(dimension_semantics=sem, vmem_limit_bytes=VMEM_LIMIT)


def _dot(a, b):
    return jnp.dot(a.astype(BF16), b.astype(BF16), preferred_element_type=F32)


def _dot_nt(a, b):
    return lax.dot_general(a.astype(BF16), b.astype(BF16), (((1,), (1,)), ((), ())),
                           preferred_element_type=F32)


def _dot_tn(a, b):
    return lax.dot_general(a.astype(BF16), b.astype(BF16), (((0,), (0,)), ((), ())),
                           preferred_element_type=F32)


def _split_dot(x, w_bf16, parts):
    acc = None
    rem = x
    for _ in range(parts):
        hi = rem.astype(BF16)
        t = jnp.dot(hi, w_bf16, preferred_element_type=F32)
        acc = t if acc is None else acc + t
        rem = rem - hi.astype(F32)
    return acc


def _silu(x):
    return x * jax.nn.sigmoid(x)


def _ada_kernel(c_ref, down_ref, up_ref, bias_ref, o_ref):
    t = _dot(_silu(c_ref[...]), down_ref[...])
    o_ref[...] = _dot(t, up_ref[...]) + bias_ref[...]


def ada_mod(c, ada_down, ada_up, ada_bias):
    L, D, R = ada_down.shape
    B = c.shape[0]
    N = ada_up.shape[-1]
    tn = D
    out = pl.pallas_call(
        _ada_kernel,
        out_shape=jax.ShapeDtypeStruct((L, B, N), F32),
        grid=(L, N // tn),
        in_specs=[
            pl.BlockSpec((B, D), lambda l, j: (0, 0)),
            pl.BlockSpec((None, D, R), lambda l, j: (l, 0, 0)),
            pl.BlockSpec((None, R, tn), lambda l, j: (l, 0, j)),
            pl.BlockSpec((None, 1, tn), lambda l, j: (l, 0, j)),
        ],
        out_specs=pl.BlockSpec((None, B, tn), lambda l, j: (l, 0, j)),
        compiler_params=_cparams(("parallel", "parallel")),
        name="ada_mod",
    )(c, ada_down, ada_up, ada_bias.reshape(L, 1, N))
    return out.reshape(L, B, 3 * N_SUB, D)


def _rms(x, g):
    return x * lax.rsqrt(jnp.mean(x * x, axis=-1, keepdims=True) + 1e-6) * g


def _normmod_kernel(x_ref, g_ref, mod_ref, h_ref, *, sub):
    shift = mod_ref[3 * sub:3 * sub + 1, :]
    scale = mod_ref[3 * sub + 1:3 * sub + 2, :]
    h_ref[0] = (_rms(x_ref[0], g_ref[...]) * (1.0 + scale) + shift).astype(h_ref.dtype)


def norm_modulate(x, gain, mod, l, sub, ts):
    B, S, D = x.shape
    return pl.pallas_call(
        functools.partial(_normmod_kernel, sub=sub),
        out_shape=jax.ShapeDtypeStruct((B, S, D), BF16),
        grid=(B, S // ts),
        in_specs=[
            pl.BlockSpec((1, ts, D), lambda b, i: (b, i, 0)),
            pl.BlockSpec((None, None, 1, D), lambda b, i: (l, sub, 0, 0)),
            pl.BlockSpec((None, None, 3 * N_SUB, D), lambda b, i: (l, b, 0, 0)),
        ],
        out_specs=pl.BlockSpec((1, ts, D), lambda b, i: (b, i, 0)),
        compiler_params=_cparams(("parallel", "parallel")),
        name="norm_modulate",
    )(x, gain, mod)


def _post_kernel(x_ref, y_ref, gpost_ref, mod_ref, gpre_ref, modn_ref, xo_ref, h_ref, *, sub, coef, nsub):
    gate = mod_ref[3 * sub + 2:3 * sub + 3, :]
    xn = x_ref[0] + coef * (gate * _rms(y_ref[0], gpost_ref[...]))
    xo_ref[0] = xn
    shift = modn_ref[3 * nsub:3 * nsub + 1, :]
    scale = modn_ref[3 * nsub + 1:3 * nsub + 2, :]
    h_ref[0] = (_rms(xn, gpre_ref[...]) * (1.0 + scale) + shift).astype(h_ref.dtype)


def _post_last_kernel(x_ref, y_ref, gpost_ref, mod_ref, xo_ref, *, sub, coef):
    gate = mod_ref[3 * sub + 2:3 * sub + 3, :]
    xo_ref[0] = x_ref[0] + coef * (gate * _rms(y_ref[0], gpost_ref[...]))


def post_residual(x, y, norm_post, norm_pre, mod, l, sub, coef, nxt, ts):
    B, S, D = x.shape
    xspec = pl.BlockSpec((1, ts, D), lambda b, i: (b, i, 0))
    gspec = lambda ll, ss: pl.BlockSpec((None, None, 1, D), lambda b, i: (ll, ss, 0, 0))
    mspec = lambda ll: pl.BlockSpec((None, None, 3 * N_SUB, D), lambda b, i: (ll, b, 0, 0))
    if nxt is None:
        return pl.pallas_call(
            functools.partial(_post_last_kernel, sub=sub, coef=coef),
            out_shape=jax.ShapeDtypeStruct((B, S, D), F32),
            grid=(B, S // ts),
            in_specs=[xspec, xspec, gspec(l, sub), mspec(l)],
            out_specs=xspec,
            compiler_params=_cparams(("parallel", "parallel")),
            name="post_last",
        )(x, y, norm_post, mod), None
    l2, sub2 = nxt
    return pl.pallas_call(
        functools.partial(_post_kernel, sub=sub, coef=coef, nsub=sub2),
        out_shape=(jax.ShapeDtypeStruct((B, S, D), F32), jax.ShapeDtypeStruct((B, S, D), BF16)),
        grid=(B, S // ts),
        in_specs=[xspec, xspec, gspec(l, sub), mspec(l), gspec(l2, sub2), mspec(l2)],
        out_specs=(xspec, xspec),
        compiler_params=_cparams(("parallel", "parallel")),
        name="post_residual",
    )(x, y, norm_post, mod, norm_pre, mod)


def _mm_kernel(a_ref, w_ref, o_ref, *, act):
    acc = jnp.dot(a_ref[...], w_ref[...], preferred_element_type=F32)
    if act == "sigmoid":
        acc = jax.nn.sigmoid(acc)
    o_ref[...] = acc.astype(o_ref.dtype)


def matmul(a, w, l, tm, tn, out_dtype, act=None, name="matmul", col0=0, n_cols=None):
    M, K = a.shape
    N = w.shape[-1] if n_cols is None else n_cols
    assert col0 % tn == 0 and N % tn == 0
    j0 = col0 // tn
    return pl.pallas_call(
        functools.partial(_mm_kernel, act=act),
        out_shape=jax.ShapeDtypeStruct((M, N), out_dtype),
        grid=(M // tm, N // tn),
        in_specs=[
            pl.BlockSpec((tm, K), lambda i, j: (i, 0)),
            pl.BlockSpec((None, K, tn), lambda i, j: (l, 0, j0 + j)),
        ],
        out_specs=pl.BlockSpec((tm, tn), lambda i, j: (i, j)),
        compiler_params=_cparams(("parallel", "arbitrary")),
        name=name,
    )(a, w)


def _swiglu_kernel(a_ref, wa_ref, wb_ref, o_ref):
    h = a_ref[...]
    a = jnp.dot(h, wa_ref[...], preferred_element_type=F32)
    b = jnp.dot(h, wb_ref[...], preferred_element_type=F32)
    o_ref[...] = (_silu(a) * b).astype(o_ref.dtype)


def swiglu_in(h, w, l, tm, tn):
    M, K = h.shape
    F = w.shape[-1] // 2
    nb = F // tn
    return pl.pallas_call(
        _swiglu_kernel,
        out_shape=jax.ShapeDtypeStruct((M, F), BF16),
        grid=(M // tm, nb),
        in_specs=[
            pl.BlockSpec((tm, K), lambda i, j: (i, 0)),
            pl.BlockSpec((None, K, tn), lambda i, j: (l, 0, j)),
            pl.BlockSpec((None, K, tn), lambda i, j: (l, 0, j + nb)),
        ],
        out_specs=pl.BlockSpec((tm, tn), lambda i, j: (i, j)),
        compiler_params=_cparams(("parallel", "arbitrary")),
        name="swiglu_in",
    )(h, w, w)


def _merge_kernel(o1_ref, o2_ref, o3_ref, w1_ref, w2_ref, w3_ref, g1_ref, g2_ref, g3_ref, o_ref):
    acc = g1_ref[...].astype(F32) * jnp.dot(o1_ref[...], w1_ref[...], preferred_element_type=F32)
    acc += g2_ref[...].astype(F32) * jnp.dot(o2_ref[...], w2_ref[...], preferred_element_type=F32)
    acc += g3_ref[...].astype(F32) * jnp.dot(o3_ref[...], w3_ref[...], preferred_element_type=F32)
    o_ref[...] = acc.astype(o_ref.dtype)


def branch_merge(o_ret, o_nsa, o_rwkv, w_ret, w_nsa, w_rwkv, zg, l, tm, tn):
    M = o_ret.shape[0]
    D = w_ret.shape[-1]
    nb = D // tn
    ospec = lambda o: pl.BlockSpec((tm, o.shape[1]), lambda i, j: (i, 0))
    wspec = lambda w: pl.BlockSpec((None, w.shape[1], tn), lambda i, j: (l, 0, j))
    gspec = lambda k: pl.BlockSpec((tm, tn), lambda i, j: (i, j + k * nb))
    return pl.pallas_call(
        _merge_kernel,
        out_shape=jax.ShapeDtypeStruct((M, D), BF16),
        grid=(M // tm, nb),
        in_specs=[ospec(o_ret), ospec(o_nsa), ospec(o_rwkv), wspec(w_ret), wspec(w_nsa), wspec(w_rwkv),
                  gspec(0), gspec(1), gspec(2)],
        out_specs=pl.BlockSpec((tm, tn), lambda i, j: (i, j)),
        compiler_params=_cparams(("parallel", "arbitrary")),
        name="branch_merge",
    )(o_ret, o_nsa, o_rwkv, w_ret, w_nsa, w_rwkv, zg, zg, zg)


def _ret_kernel(q_ref, k_ref, v_ref, g_ref, cos_ref, sin_ref, dm_ref, zeta_ref, xi_ref, o_ref, r_scr, *, decays):
    @pl.when(pl.program_id(1) == 0)
    def _():
        r_scr[...] = jnp.zeros_like(r_scr)

    cos = cos_ref[...]
    sin = sin_ref[...]
    dk, dv = RET_QK_HD, RET_V_HD
    for h in range(RET_HEADS):
        qh = q_ref[0, :, h * dk:(h + 1) * dk]
        kh = k_ref[0, :, h * dk:(h + 1) * dk]
        qh = qh * cos + pltpu.roll(qh, dk // 2, 1) * sin
        kh = (kh * cos + pltpu.roll(kh, dk // 2, 1) * sin) * (dk ** -0.5)
        vh = v_ref[0, :, h * dv:(h + 1) * dv]
        s = _dot_nt(qh, kh) * dm_ref[h]
        state = r_scr[h]
        o = _dot(s, vh) + _dot(qh, state) * xi_ref[h]
        mu = jnp.mean(o, axis=-1, keepdims=True)
        d = o - mu
        var = jnp.mean(d * d, axis=-1, keepdims=True)
        on = d * lax.rsqrt(var + 1e-6)
        gh = g_ref[0, :, h * dv:(h + 1) * dv]
        o_ref[0, :, h * dv:(h + 1) * dv] = (_silu(gh) * on).astype(o_ref.dtype)
        r_scr[h] = _dot_tn(kh * zeta_ref[h], vh) + decays[h] * state


def retention(z3):
    B, S, _ = z3.shape
    H, C = RET_HEADS, RET_CHUNK
    pos = jnp.arange(S, dtype=F32)
    inv = 1.0 / (ROPE_BASE ** jnp.linspace(0.0, 1.0, RET_QK_HD // 2))
    ang = pos[:, None] * inv[None, :]
    cos, sin = jnp.cos(ang), jnp.sin(ang)
    cos_f = jnp.concatenate([cos, cos], axis=-1)
    sin_f = jnp.concatenate([-sin, sin], axis=-1)
    log_g = jnp.log1p(-(2.0 ** (-5.0 - jnp.arange(H, dtype=F32))))
    idx = jnp.arange(C, dtype=F32)
    diff = idx[:, None] - idx[None, :]
    dmask = jnp.where(diff >= 0, jnp.exp(jnp.maximum(diff, 0.0)[None] * log_g[:, None, None]), 0.0)
    zeta = jnp.exp((C - 1 - idx)[None, :] * log_g[:, None])
    xi = jnp.exp((idx + 1)[None, :] * log_g[:, None])
    zeta_t = jnp.broadcast_to(zeta[:, :, None], (H, C, RET_QK_HD))
    xi_t = jnp.broadcast_to(xi[:, :, None], (H, C, RET_V_HD))
    lg64 = np.log1p(-(2.0 ** (-5.0 - np.arange(H, dtype=np.float64))))
    decays = tuple(float(v) for v in np.exp(C * lg64))
    qb = ZM_RET // RET_QK
    vb = (ZM_RET + 2 * RET_QK) // RET_V
    const = lambda shape: pl.BlockSpec(shape, lambda b, c: (0,) * len(shape))
    return pl.pallas_call(
        functools.partial(_ret_kernel, decays=decays),
        out_shape=jax.ShapeDtypeStruct((B, S, RET_V), BF16),
        grid=(B, S // C),
        in_specs=[
            pl.BlockSpec((1, C, RET_QK), lambda b, c: (b, c, qb)),
            pl.BlockSpec((1, C, RET_QK), lambda b, c: (b, c, qb + 1)),
            pl.BlockSpec((1, C, RET_V), lambda b, c: (b, c, vb)),
            pl.BlockSpec((1, C, RET_V), lambda b, c: (b, c, vb + 1)),
            pl.BlockSpec((C, RET_QK_HD), lambda b, c: (c, 0)),
            pl.BlockSpec((C, RET_QK_HD), lambda b, c: (c, 0)),
            const((H, C, C)), const((H, C, RET_QK_HD)), const((H, C, RET_V_HD)),
        ],
        out_specs=pl.BlockSpec((1, C, RET_V), lambda b, c: (b, c, 0)),
        scratch_shapes=[pltpu.VMEM((H, RET_QK_HD, RET_V_HD), F32)],
        compiler_params=_cparams(("parallel", "arbitrary")),
        name="retention",
    )(z3, z3, z3, z3, cos_f, sin_f, dmask, zeta_t, xi_t)


def _rel_bucket(dist):
    n = jnp.maximum(dist, 0)
    max_exact = REL_BUCKETS // 2
    nf = jnp.maximum(n, 1).astype(F32)
    large = max_exact + (jnp.log(nf / max_exact) / math.log(REL_MAX_DIST / max_exact)
                         * (REL_BUCKETS - max_exact)).astype(jnp.int32)
    large = jnp.minimum(large, REL_BUCKETS - 1)
    return jnp.where(n < max_exact, n, large)


def _cmp_kernel(kc_ref, vc_ref, pos_ref, w1_ref, b1_ref, w2_ref, b2_ref, ko_ref, vo_ref, *, nb):
    d = NSA_HD
    half = CMP_BLOCK // 2
    for i, (src, dst) in enumerate(((kc_ref, ko_ref), (vc_ref, vo_ref))):
        p1 = jnp.zeros((nb, w1_ref.shape[-1]), F32)
        p2 = jnp.zeros((nb, w1_ref.shape[-1]), F32)
        for t in range(half):
            a = src[0, pl.ds(t, nb, stride=CMP_STRIDE), :]
            p1 += _dot(a + pos_ref[i, t:t + 1, :], w1_ref[i, t * d:(t + 1) * d, :])
            p2 += _dot(a + pos_ref[i, half + t:half + t + 1, :], w1_ref[i, (half + t) * d:(half + t + 1) * d, :])
        pre = p1 + pltpu.roll(p2, nb - 1, 0) + b1_ref[i]
        dst[0, 0] = _dot(jax.nn.gelu(pre), w2_ref[i]) + b2_ref[i]


def nsa_compress(z3, cmp_pos, cmp_w1, cmp_b1, cmp_w2, cmp_b2):
    B, S, _ = z3.shape
    G, d = NSA_KV_HEADS, NSA_HD
    nb = S // CMP_STRIDE
    kb = ZM_NSA_KV // d
    hid = cmp_w1.shape[-1]
    const = lambda shape: pl.BlockSpec(shape, lambda b, g: (0,) * len(shape))
    out = jax.ShapeDtypeStruct((B, G, nb, d), F32)
    return pl.pallas_call(
        functools.partial(_cmp_kernel, nb=nb),
        out_shape=(out, out),
        grid=(B, G),
        in_specs=[
            pl.BlockSpec((1, S, d), lambda b, g: (b, 0, kb + g)),
            pl.BlockSpec((1, S, d), lambda b, g: (b, 0, kb + G + g)),
            const((2, CMP_BLOCK, d)), const((2, CMP_BLOCK * d, hid)), const((2, 1, hid)),
            const((2, hid, d)), const((2, 1, d)),
        ],
        out_specs=(pl.BlockSpec((1, 1, nb, d), lambda b, g: (b, g, 0, 0)),
                   pl.BlockSpec((1, 1, nb, d), lambda b, g: (b, g, 0, 0))),
        compiler_params=_cparams(("parallel", "parallel")),
        name="nsa_compress",
    )(z3, z3, cmp_pos, cmp_w1, cmp_b1.reshape(2, 1, hid), cmp_w2, cmp_b2.reshape(2, 1, d))


def _nsa_kernel(q_ref, gate_ref, kc_ref, vc_ref, ks_ref, vs_ref, kw_ref, vw_ref, bc_ref, bs_ref, bw_ref, ov_ref, e_ref,
                o_ref, q_scr, ksf, vsf, kwf, vwf, *, n_sblk, nkb):
    g = pl.program_id(1)
    qi = pl.program_id(2)
    T = LANES
    HG = NSA_GROUP
    nwb = WINDOW // T + 1
    spb = T // SEL_BLOCK
    row = lax.broadcasted_iota(jnp.int32, (T, T), 0)
    col = lax.broadcasted_iota(jnp.int32, (T, T), 1)
    qpos = qi * T + row

    @pl.when(qi == 0)
    def _():
        for src, dst in ((ks_ref, ksf), (vs_ref, vsf), (kw_ref, kwf), (vw_ref, vwf)):
            for blk in range(nkb):
                dst[(nkb - 1 - blk) * T:(nkb - blk) * T, :] = src[0, blk * T:(blk + 1) * T, :].astype(BF16)
            dst[nkb * T:, :] = jnp.zeros(((nkb - 1) * T, NSA_HD), BF16)

    for j in range(HG):
        q_scr[j * T:(j + 1) * T, :] = (q_ref[0, :, j * T:(j + 1) * T] * (NSA_HD ** -0.5)).astype(BF16)
    q_all = q_scr[...]
    start = pl.multiple_of((nkb - 1 - qi) * T, T)

    def softmax_pv(s, v):
        m = jnp.max(s, axis=-1, keepdims=True)
        p = jnp.exp(s - m)
        den = jnp.sum(p, axis=-1, keepdims=True)
        o = jnp.dot(p.reshape(HG * T, p.shape[-1]).astype(BF16), v, preferred_element_type=F32)
        return o.reshape(HG, T, v.shape[-1]) / den

    s = _dot_nt(q_all, kc_ref[0, 0]).reshape(HG, T, T) + bc_ref[...]
    m = jnp.max(s, axis=-1, keepdims=True)
    p = jnp.where(s > 0.5 * NEG, jnp.exp(s - m), 0.0)
    den = jnp.sum(p, axis=-1, keepdims=True)
    pn = p / jnp.where(den > 0.0, den, 1.0)
    o_c = _dot(pn.reshape(HG * T, T), vc_ref[0, 0]).reshape(HG, T, NSA_HD)
    psum = pn[0]
    for j in range(1, HG):
        psum = psum + pn[j]

    imp = _split_dot(psum, ov_ref[...], 2)
    cur = qpos // SEL_BLOCK
    forced = (col == 0) | (col == cur) | (col == cur - 1)
    impm = jnp.where(forced, 1e30, jnp.where(col <= cur, imp, NEG))
    rank = jnp.zeros((T, T), F32)
    for mb in range(n_sblk):
        cm = impm[:, mb:mb + 1]
        lower = jnp.where(col > mb, 1.0, 0.0)
        rank += jnp.where(cm > impm, 1.0, jnp.where(cm == impm, lower, 0.0))
    sel = jnp.where(rank < float(min(N_SEL, n_sblk)), 1.0, 0.0).astype(BF16)

    rev = jnp.where(row + col == spb * qi + spb - 1, 1.0, 0.0).astype(BF16)
    sel_back = jnp.dot(sel, rev, preferred_element_type=F32).astype(BF16)
    keep = jnp.dot(sel_back, e_ref[...], preferred_element_type=F32)
    s = _dot_nt(q_all, ksf[pl.ds(start, nkb * T), :]).reshape(HG, T, nkb * T) + bs_ref[...]
    o_s = softmax_pv(s + ((keep - 1.0) * -NEG)[None], vsf[pl.ds(start, nkb * T), :])

    colw = lax.broadcasted_iota(jnp.int32, (T, nwb * T), 1)
    s = _dot_nt(q_all, kwf[pl.ds(start, nwb * T), :]).reshape(HG, T, nwb * T) + bw_ref[...]
    o_w = softmax_pv(s + jnp.where(colw < (qi + 1) * T, 0.0, NEG)[None], vwf[pl.ds(start, nwb * T), :])

    gates = jax.nn.sigmoid(gate_ref[0])
    for j in range(HG):
        base = 3 * j
        gsel = lambda c: jnp.where(g == 0, gates[:, c:c + 1], gates[:, 3 * HG + c:3 * HG + c + 1])
        o = gsel(base) * o_c[j] + gsel(base + 1) * o_s[j] + gsel(base + 2) * o_w[j]
        o_ref[0, :, j * T:(j + 1) * T] = o.astype(o_ref.dtype)


def nsa_tables(rel_bias, S):
    T = LANES
    n_cmp = (S - CMP_BLOCK) // CMP_STRIDE + 1
    n_sblk = S // SEL_BLOCK
    nkb = S // T
    nwb = WINDOW // T + 1
    assert n_cmp < T and n_sblk <= T and S // CMP_STRIDE == T and nwb <= nkb
    tab = rel_bias[_rel_bucket(jnp.arange(S + T)), :].T
    ii = jnp.arange(T)[:, None]
    cc = jnp.arange(nkb * T)[None, :]
    dist = ii + T * (cc // T) - cc % T
    bias = tab[:, jnp.maximum(dist, 0)]
    bias_s = jnp.where(dist >= 0, bias, NEG)
    bias_w = jnp.where((dist >= 0) & (dist < WINDOW), bias, NEG)[:, :, :nwb * T]
    dist_c = jnp.arange(S)[:, None] - (CMP_STRIDE * jnp.arange(T)[None, :] + CMP_BLOCK - 1)
    valid_c = (dist_c >= 0) & (jnp.arange(T)[None, :] < n_cmp)
    bias_c = jnp.where(valid_c, rel_bias[_rel_bucket(dist_c), :].transpose(2, 0, 1), NEG)

    cmp_start = CMP_STRIDE * np.arange(T)
    sel_start = SEL_BLOCK * np.arange(T)
    ov = ((cmp_start[:, None] <= (sel_start + SEL_BLOCK - 1)[None, :])
          & ((cmp_start + CMP_BLOCK - 1)[:, None] >= sel_start[None, :])
          & (np.arange(T)[:, None] < n_cmp) & (np.arange(T)[None, :] < n_sblk))
    spb = T // SEL_BLOCK
    c = np.arange(nkb * T)
    back = spb * (c // T) + spb - 1 - (c % T) // SEL_BLOCK
    expand = np.arange(T)[:, None] == back[None, :]
    return bias_c, bias_s, bias_w, jnp.asarray(ov, BF16), jnp.asarray(expand, BF16)


def nsa_attention(zm, zt, kcmp, vcmp, tables):
    B, S, _ = zm.shape
    G, HG, d, T = NSA_KV_HEADS, NSA_GROUP, NSA_HD, LANES
    assert G == 2 and d == T
    n_sblk = S // SEL_BLOCK
    nkb = S // T
    nwb = WINDOW // T + 1
    bias_c, bias_s, bias_w, ov, expand = tables

    qb = ZM_NSA_Q // (HG * d)
    kvb = ZM_NSA_KV // d
    kvspec = lambda off: pl.BlockSpec((1, S, d), lambda b, g, i: (b, 0, kvb + off * G + g))
    cspec = pl.BlockSpec((1, 1, T, d), lambda b, g, i: (b, g, 0, 0))
    flip = pltpu.VMEM(((2 * nkb - 1) * T, d), BF16)
    return pl.pallas_call(
        functools.partial(_nsa_kernel, n_sblk=n_sblk, nkb=nkb),
        out_shape=jax.ShapeDtypeStruct((B, S, NSA_Q), BF16),
        grid=(B, G, nkb),
        in_specs=[
            pl.BlockSpec((1, T, HG * d), lambda b, g, i: (b, i, qb + g)),
            pl.BlockSpec((1, T, NSA_GATE_PAD), lambda b, g, i: (b, i, ZT_GATE // NSA_GATE_PAD)),
            cspec, cspec, kvspec(2), kvspec(3), kvspec(4), kvspec(5),
            pl.BlockSpec((HG, T, T), lambda b, g, i: (g, i, 0)),
            pl.BlockSpec((HG, T, nkb * T), lambda b, g, i: (g, 0, 0)),
            pl.BlockSpec((HG, T, nwb * T), lambda b, g, i: (g, 0, 0)),
            pl.BlockSpec((T, T), lambda b, g, i: (0, 0)),
            pl.BlockSpec((T, nkb * T), lambda b, g, i: (0, 0)),
        ],
        out_specs=pl.BlockSpec((1, T, HG * d), lambda b, g, i: (b, i, g)),
        scratch_shapes=[pltpu.VMEM((HG * T, d), BF16), flip, flip, flip, flip],
        compiler_params=_cparams(("parallel", "parallel", "arbitrary")),
        name="nsa_attention",
    )(zm, zt, kcmp, vcmp, zm, zm, zm, zm, bias_c, bias_s, bias_w, ov, expand)


def _rwkv_prep_kernel(zr_ref, zk_ref, zv_ref, zl_ref, mu_ref, mul_ref, vec_ref, w2_ref, a2_ref, g2_ref,
                      r_ref, k_ref, v_ref, lw_ref, kk_ref, kka_ref, gg_ref, c_scr, cl_scr):
    first = pl.program_id(1) == 0
    tb = zr_ref.shape[1]
    row = lax.broadcasted_iota(jnp.int32, (tb, 1), 0)

    def shift(z, mu, carry_ref, slot):
        prev_last = jnp.where(first, 0.0, carry_ref[slot:slot + 1, :])
        prev = jnp.where(row == 0, prev_last, pltpu.roll(z, 1, 0))
        carry_ref[slot:slot + 1, :] = z[tb - 1:tb, :]
        return z + (prev - z) * mu

    r = shift(zr_ref[0], mu_ref[0:1, :], c_scr, 0)
    k = shift(zk_ref[0], mu_ref[1:2, :], c_scr, 1)
    v = shift(zv_ref[0], mu_ref[2:3, :], c_scr, 2)
    xl = shift(zl_ref[0], mul_ref[...], cl_scr, 0)
    w0, a0, k_k, k_a = vec_ref[0:1, :], vec_ref[1:2, :], vec_ref[2:3, :], vec_ref[3:4, :]
    u = -(w0 + _dot(jnp.tanh(xl), w2_ref[...]))
    w_log = -(jnp.maximum(u, 0.0) + jnp.log1p(jnp.exp(-jnp.abs(u)))) - 0.5
    a = jax.nn.sigmoid(a0 + _dot(xl, a2_ref[...]))
    r_ref[0] = r
    k_ref[0] = k * (1.0 + (a - 1.0) * k_a)
    v_ref[0] = v
    lw_ref[0] = -jnp.exp(w_log)
    kk = k * k_k
    kk_ref[0] = kk
    kka_ref[0] = kk * a
    gg_ref[0] = _dot(jax.nn.sigmoid(xl), g2_ref[...])


def rwkv_prep(z3, mu_rkv, mu_lora, vecs, w2p, a2p, g2p, tb):
    B, S, _ = z3.shape
    Dm = RWKV_DIM
    rb = ZT_RWKV // Dm
    zspec = lambda o: pl.BlockSpec((1, tb, Dm), lambda b, i: (b, i, rb + o))
    const = lambda shape: pl.BlockSpec(shape, lambda b, i: (0,) * len(shape))
    ospec = pl.BlockSpec((1, tb, Dm), lambda b, i: (b, i, 0))
    out = jax.ShapeDtypeStruct((B, S, Dm), F32)
    return pl.pallas_call(
        _rwkv_prep_kernel,
        out_shape=(out,) * 7,
        grid=(B, S // tb),
        in_specs=[zspec(0), zspec(1), zspec(2),
                  pl.BlockSpec((1, tb, LORA_PAD), lambda b, i: (b, i, ZT_LORA // LORA_PAD)),
                  const((3, Dm)), const((1, LORA_PAD)), const((4, Dm)),
                  const((LORA_PAD, Dm)), const((LORA_PAD, Dm)), const((LORA_PAD, Dm))],
        out_specs=(ospec,) * 7,
        scratch_shapes=[pltpu.VMEM((8, Dm), F32), pltpu.VMEM((8, LORA_PAD), F32)],
        compiler_params=_cparams(("parallel", "arbitrary")),
        name="rwkv_prep",
    )(z3, z3, z3, z3, mu_rkv, mu_lora, vecs, w2p, a2p, g2p)


def _rwkv_masks():
    T, n = RWKV_CHUNK, RWKV_GW
    idx = np.arange(n)
    h, t = idx // T, idx % T
    same = h[:, None] == h[None, :]
    tt, ss = t[:, None], t[None, :]
    levels = []
    b = 1
    while b < T:
        levels.append(same & (tt // (2 * b) == ss // (2 * b)) & (tt % (2 * b) >= b) & (ss % (2 * b) < b))
        b *= 2
    masks = [same, same & (ss < tt), same & (ss <= tt), np.eye(n, dtype=bool)] + levels
    return np.stack(masks).astype(np.float32), len(levels)


def _rwkv_chunk_kernel(r_ref, k_ref, v_ref, lw_ref, kk_ref, kka_ref, gg_ref, rk_ref, ln_ref, msk_ref, tri_ref,
                       o_ref, s_scr, *, n_levels):
    T, GW = RWKV_CHUNK, RWKV_GW

    @pl.when(pl.program_id(1) == 0)
    def _():
        s_scr[...] = jnp.zeros_like(s_scr)

    m_bd = msk_ref[0]
    m_strict = msk_ref[1]
    m_incl = msk_ref[2]
    eye = msk_ref[3].astype(F32)
    tri = tri_ref[...]

    def to_bd(x):
        return jnp.concatenate([x.astype(BF16)] * RWKV_GROUP, axis=0) * m_bd

    def from_bd(y):
        out = y[0:T]
        for i in range(1, RWKV_GROUP):
            out = out + y[i * T:(i + 1) * T]
        return out

    def seg_sum(x):
        return _split_dot(x, m_bd, 2)

    def mm(a, b):
        return jnp.dot(a, b, preferred_element_type=F32)

    def mm_nt(a, b):
        return lax.dot_general(a, b, (((1,), (1,)), ((), ())), preferred_element_type=F32)

    def mm_tn(a, b):
        return lax.dot_general(a, b, (((0,), (0,)), ((), ())), preferred_element_type=F32)

    groups = range(RWKV_DIM // GW)
    sls = [slice(gi * GW, (gi + 1) * GW) for gi in groups]
    rs = [r_ref[0, :, sl] for sl in sls]
    ks = [k_ref[0, :, sl] for sl in sls]
    vs = [v_ref[0, :, sl] for sl in sls]
    lws = [lw_ref[0, :, sl] for sl in sls]
    inv_n = [1.0 / jnp.maximum(jnp.sqrt(seg_sum(kk_ref[0, :, sl] * kk_ref[0, :, sl])), 1e-12) for sl in sls]
    cum = [_split_dot_left(tri, lw) for lw in lws]
    p_in = [jnp.exp(c) for c in cum]
    p_inv = [jnp.exp(-c) for c in cum]
    a_bd = [to_bd(-(kk_ref[0, :, sl] * n) * jnp.exp(c - lw)) for sl, n, c, lw in zip(sls, inv_n, cum, lws)]
    r_bd = [to_bd(r * p) for r, p in zip(rs, p_in)]
    b_bd = [to_bd(kka_ref[0, :, sl] * n * p) for sl, n, p in zip(sls, inv_n, p_inv)]
    k_bd = [to_bd(k * p) for k, p in zip(ks, p_inv)]
    v_bd = [to_bd(v) for v in vs]

    a_ab = [(mm_nt(a, b) * m_strict).astype(BF16) for a, b in zip(a_bd, b_bd)]
    a_ak = [(mm_nt(a, k) * m_strict).astype(BF16) for a, k in zip(a_bd, k_bd)]
    a_rb = [(mm_nt(r, b) * m_incl).astype(BF16) for r, b in zip(r_bd, b_bd)]
    a_rk = [(mm_nt(r, k) * m_incl).astype(BF16) for r, k in zip(r_bd, k_bd)]

    x = [eye + a * msk_ref[4] for a in a_ab]
    for lv in range(1, n_levels):
        xb = [xi.astype(BF16) for xi in x]
        t = [mm(xi, a * msk_ref[4 + lv]).astype(BF16) for xi, a in zip(xb, a_ab)]
        x = [xf + mm(ti, xi) for xf, ti, xi in zip(x, t, xb)]
    xb = [xi.astype(BF16) for xi in x]

    state = [s_scr[gi] for gi in groups]
    sb = [s.astype(BF16) for s in state]
    rhs = [(mm_nt(a, s) + mm(ak, v)).astype(BF16) for a, s, ak, v in zip(a_bd, sb, a_ak, v_bd)]
    u = [mm(xi, q).astype(BF16) for xi, q in zip(xb, rhs)]
    y = [from_bd(mm_nt(r, s) + mm(arb, ui) + mm(ark, v))
         for r, s, arb, ui, ark, v in zip(r_bd, sb, a_rb, u, a_rk, v_bd)]
    for gi in groups:
        s_scr[gi] = (state[gi] + mm_tn(u[gi], b_bd[gi]) + mm_tn(v_bd[gi], k_bd[gi])) * p_in[gi][T - 1:T, :]

    mu = [seg_sum(yi) * (1.0 / RWKV_HD) for yi in y]
    d = [yi - m for yi, m in zip(y, mu)]
    var = [seg_sum(di * di) * (1.0 / RWKV_HD) for di in d]
    bonus = [seg_sum(r * k * rk_ref[:, sl]) * v for r, k, v, sl in zip(rs, ks, vs, sls)]
    for gi, sl in enumerate(sls):
        yn = d[gi] * lax.rsqrt(var[gi] + RWKV_LN_EPS) * ln_ref[0:1, sl] + ln_ref[1:2, sl]
        o_ref[0, :, sl] = ((yn + bonus[gi]) * gg_ref[0, :, sl]).astype(o_ref.dtype)


def _split_dot_left(w_bf16, x):
    acc = None
    rem = x
    for _ in range(3):
        hi = rem.astype(BF16)
        t = jnp.dot(w_bf16, hi, preferred_element_type=F32)
        acc = t if acc is None else acc + t
        rem = rem - hi.astype(F32)
    return acc


def rwkv_chunk(r, k, v, lw, kk, kka, gg, rk, ln):
    B, S, Dm = r.shape
    T = RWKV_CHUNK
    masks, n_levels = _rwkv_masks()
    tri = np.tril(np.ones((T, T), np.float32))
    ng = Dm // RWKV_GW
    xspec = pl.BlockSpec((1, T, Dm), lambda b, c: (b, c, 0))
    const = lambda shape: pl.BlockSpec(shape, lambda b, c: (0,) * len(shape))
    return pl.pallas_call(
        functools.partial(_rwkv_chunk_kernel, n_levels=n_levels),
        out_shape=jax.ShapeDtypeStruct((B, S, Dm), BF16),
        grid=(B, S // T),
        in_specs=[xspec] * 7 + [const((1, Dm)), const((2, Dm)), const(masks.shape), const((T, T))],
        out_specs=xspec,
        scratch_shapes=[pltpu.VMEM((ng, RWKV_GW, RWKV_GW), F32)],
        compiler_params=_cparams(("parallel", "arbitrary")),
        name="rwkv_chunk",
    )(r, k, v, lw, kk, kka, gg, rk, ln, jnp.asarray(masks, BF16), jnp.asarray(tri, BF16))


def _prep_weights(w_in, rwkv_mu, rwkv_w2, rwkv_a2, rwkv_g2):
    L, D, _ = w_in.shape
    o = 3 * D + ZM_COLS
    w_main = w_in[:, :, :o].astype(BF16)
    nsa_gate = w_in[:, :, o:o + 3 * NSA_HEADS]
    o += 3 * NSA_HEADS
    rkv = w_in[:, :, o:o + 3 * RWKV_DIM]
    lora = w_in[:, :, o + 3 * RWKV_DIM:o + RWKV_COLS]
    pad = lambda w, n: jnp.pad(w, ((0, 0), (0, 0), (0, n - w.shape[-1])))
    w_tail = jnp.concatenate([rkv, pad(lora, LORA_PAD), pad(nsa_gate, NSA_GATE_PAD)], axis=-1).astype(BF16)
    mu_rkv = rwkv_mu[:, :3 * RWKV_DIM].reshape(L, 3, RWKV_DIM)
    mu_lora = jnp.pad(rwkv_mu[:, 3 * RWKV_DIM:], ((0, 0), (0, LORA_PAD - LORA_COLS))).reshape(L, 1, LORA_PAD)
    rows = lambda w, start: jnp.pad(w, ((0, 0), (start, LORA_PAD - start - w.shape[1]), (0, 0)))
    w2p = rows(rwkv_w2, 0)
    a2p = rows(rwkv_a2, DECAY_LORA)
    g2p = rows(rwkv_g2, DECAY_LORA + AAA_LORA)
    return w_main, w_tail, mu_rkv, mu_lora, w2p, a2p, g2p


def kernel(x, c, rel_bias, w_in, w_branch_ret, w_branch_nsa, w_branch_rwkv, w_out, ffn1_in, ffn1_out, ffn2_in,
           ffn2_out, ada_down, ada_up, ada_bias, norm_pre, norm_post, cmp_pos, cmp_w1, cmp_b1, cmp_w2, cmp_b2,
           rwkv_mu, rwkv_vecs, rwkv_w2, rwkv_a2, rwkv_g2, rwkv_rk, rwkv_ln):
    B, S, D = x.shape
    L = w_in.shape[0]
    M = B * S
    tm = min(1024, M)
    ts = min(256, S)
    tb = min(256, S)

    w_main, w_tail, mu_rkv, mu_lora, w2p, a2p, g2p = _prep_weights(w_in, rwkv_mu, rwkv_w2, rwkv_a2, rwkv_g2)
    tables = nsa_tables(rel_bias, S)
    w_bret, w_bnsa, w_brwkv = w_branch_ret.astype(BF16), w_branch_nsa.astype(BF16), w_branch_rwkv.astype(BF16)
    w_o = w_out.astype(BF16)
    f1i, f1o, f2i, f2o = (w.astype(BF16) for w in (ffn1_in, ffn1_out, ffn2_in, ffn2_out))
    npre = norm_pre.reshape(L, N_SUB, 1, D)
    npost = norm_post.reshape(L, N_SUB, 1, D)
    rk = rwkv_rk.reshape(L, 1, RWKV_DIM)

    mod = ada_mod(c, ada_down, ada_up, ada_bias)
    h = norm_modulate(x, npre, mod, 0, 0, ts)
    for l in range(L):
        u = swiglu_in(h.reshape(M, D), f1i, l, tm, 512)
        y = matmul(u, f1o, l, tm, 512, F32, name="ffn_out")
        x, h = post_residual(x, y.reshape(B, S, D), npost, npre, mod, l, 0, 0.5, (l, 1), ts)
        h2 = h.reshape(M, D)
        zg = matmul(h2, w_main, l, tm, 512, BF16, act="sigmoid", name="gate_proj", n_cols=3 * D)
        zm = matmul(h2, w_main, l, tm, 512, F32, name="mix_proj", col0=3 * D, n_cols=ZM_COLS).reshape(B, S, ZM_COLS)
        zt = matmul(h2, w_tail, l, tm, 512, F32, name="tail_proj").reshape(B, S, ZT_COLS)
        o_ret = retention(zm)
        kcmp, vcmp = nsa_compress(zm, cmp_pos[l], cmp_w1[l], cmp_b1[l], cmp_w2[l], cmp_b2[l])
        o_nsa = nsa_attention(zm, zt, kcmp, vcmp, tables)
        rw = rwkv_prep(zt, mu_rkv[l], mu_lora[l], rwkv_vecs[l], w2p[l], a2p[l], g2p[l], tb)
        o_rwkv = rwkv_chunk(*rw, rk[l], rwkv_ln[l])
        merged = branch_merge(o_ret.reshape(M, RET_V), o_nsa.reshape(M, NSA_Q), o_rwkv.reshape(M, RWKV_DIM),
                              w_bret, w_bnsa, w_brwkv, zg, l, tm, 512)
        y = matmul(merged, w_o, l, tm, 512, F32, name="out_proj")
        x, h = post_residual(x, y.reshape(B, S, D), npost, npre, mod, l, 1, 1.0, (l, 2), ts)
        u = swiglu_in(h.reshape(M, D), f2i, l, tm, 512)
        y = matmul(u, f2o, l, tm, 512, F32, name="ffn_out")
        nxt = (l + 1, 0) if l + 1 < L else None
        x, h = post_residual(x, y.reshape(B, S, D), npost, npre, mod, l, 2, 0.5, nxt, ts)
    return x
```

```python
import functools
import math

import numpy as np
import jax
import jax.numpy as jnp
from jax import lax
from jax.experimental import pallas as pl
from jax.experimental.pallas import tpu as pltpu

F32 = jnp.float32
BF16 = jnp.bfloat16

D_MODEL = 4096
DEPTH = 4
D_FF = 3072
N_SUB = 3
RET_HEADS, RET_QK_HD, RET_V_HD, RET_CHUNK = 8, 128, 256, 128
RET_QK = RET_HEADS * RET_QK_HD
RET_V = RET_HEADS * RET_V_HD
ROPE_BASE = 10000.0
NSA_HEADS, NSA_KV_HEADS, NSA_HD = 8, 2, 128
NSA_GROUP = NSA_HEADS // NSA_KV_HEADS
NSA_Q = NSA_HEADS * NSA_HD
NSA_KV = NSA_KV_HEADS * NSA_HD
CMP_BLOCK, CMP_STRIDE = 32, 16
SEL_BLOCK, N_SEL = 64, 8
WINDOW = 512
RWKV_HD, RWKV_DIM = 64, 1024
RWKV_HEADS = RWKV_DIM // RWKV_HD
DECAY_LORA, AAA_LORA, GATE_LORA = 64, 64, 160
RWKV_LN_EPS = 64e-5
REL_BUCKETS, REL_MAX_DIST = 32, 128
RET_COLS = 2 * RET_QK + 2 * RET_V
NSA_COLS = NSA_Q + 6 * NSA_KV + 3 * NSA_HEADS
RWKV_COLS = 3 * RWKV_DIM + DECAY_LORA + AAA_LORA + GATE_LORA
NEG = -1e30

LANES = 128
VMEM_LIMIT = 56 * 1024 * 1024

ZM_RET = 0
ZM_NSA_Q = RET_COLS
ZM_NSA_KV = ZM_NSA_Q + NSA_Q
ZM_COLS = ZM_NSA_KV + 6 * NSA_KV
ZT_RWKV = 0
ZT_LORA = 3 * RWKV_DIM
LORA_COLS = DECAY_LORA + AAA_LORA + GATE_LORA
LORA_PAD = 3 * LANES
ZT_GATE = ZT_LORA + LORA_PAD
NSA_GATE_PAD = LANES
ZT_COLS = ZT_GATE + NSA_GATE_PAD

RWKV_CHUNK = 64
RWKV_GROUP = 4
RWKV_GW = RWKV_GROUP * RWKV_HD


def _col_tile(n, pref):
    return max(t for t in range(LANES, min(n, pref) + 1, LANES) if n % t == 0)


def _cparams(sem):
    return pltpu.CompilerParams(dimension_semantics=sem, vmem_limit_bytes=VMEM_LIMIT)


def _dot(a, b):
    return jnp.dot(a.astype(BF16), b.astype(BF16), preferred_element_type=F32)


def _dot_nt(a, b):
    return lax.dot_general(a.astype(BF16), b.astype(BF16), (((1,), (1,)), ((), ())),
                           preferred_element_type=F32)


def _dot_tn(a, b):
    return lax.dot_general(a.astype(BF16), b.astype(BF16), (((0,), (0,)), ((), ())),
                           preferred_element_type=F32)


def _split_dot(x, w_bf16, parts):
    acc = None
    rem = x
    for _ in range(parts):
        hi = rem.astype(BF16)
        t = jnp.dot(hi, w_bf16, preferred_element_type=F32)
        acc = t if acc is None else acc + t
        rem = rem - hi.astype(F32)
    return acc


def _silu(x):
    return x * jax.nn.sigmoid(x)


def _ada_kernel(c_ref, down_ref, up_ref, bias_ref, o_ref):
    t = _dot(_silu(c_ref[...]), down_ref[...])
    o_ref[...] = _dot(t, up_ref[...]) + bias_ref[...]


def ada_mod(c, ada_down, ada_up, ada_bias):
    L, D, R = ada_down.shape
    B = c.shape[0]
    N = ada_up.shape[-1]
    tn = D
    out = pl.pallas_call(
        _ada_kernel,
        out_shape=jax.ShapeDtypeStruct((L, B, N), F32),
        grid=(L, N // tn),
        in_specs=[
            pl.BlockSpec((B, D), lambda l, j: (0, 0)),
            pl.BlockSpec((None, D, R), lambda l, j: (l, 0, 0)),
            pl.BlockSpec((None, R, tn), lambda l, j: (l, 0, j)),
            pl.BlockSpec((None, 1, tn), lambda l, j: (l, 0, j)),
        ],
        out_specs=pl.BlockSpec((None, B, tn), lambda l, j: (l, 0, j)),
        compiler_params=_cparams(("parallel", "parallel")),
        name="ada_mod",
    )(c, ada_down, ada_up, ada_bias.reshape(L, 1, N))
    return out.reshape(L, B, 3 * N_SUB, D)


def _rms(x, g):
    return x * lax.rsqrt(jnp.mean(x * x, axis=-1, keepdims=True) + 1e-6) * g


def _normmod_kernel(x_ref, g_ref, mod_ref, h_ref, *, sub):
    shift = mod_ref[3 * sub:3 * sub + 1, :]
    scale = mod_ref[3 * sub + 1:3 * sub + 2, :]
    h_ref[0] = (_rms(x_ref[0], g_ref[...]) * (1.0 + scale) + shift).astype(h_ref.dtype)


def norm_modulate(x, gain, mod, l, sub, ts):
    B, S, D = x.shape
    return pl.pallas_call(
        functools.partial(_normmod_kernel, sub=sub),
        out_shape=jax.ShapeDtypeStruct((B, S, D), BF16),
        grid=(B, S // ts),
        in_specs=[
            pl.BlockSpec((1, ts, D), lambda b, i: (b, i, 0)),
            pl.BlockSpec((None, None, 1, D), lambda b, i: (l, sub, 0, 0)),
            pl.BlockSpec((None, None, 3 * N_SUB, D), lambda b, i: (l, b, 0, 0)),
        ],
        out_specs=pl.BlockSpec((1, ts, D), lambda b, i: (b, i, 0)),
        compiler_params=_cparams(("parallel", "parallel")),
        name="norm_modulate",
    )(x, gain, mod)


def _post_kernel(x_ref, y_ref, gpost_ref, mod_ref, gpre_ref, modn_ref, xo_ref, h_ref, *, sub, coef, nsub):
    gate = mod_ref[3 * sub + 2:3 * sub + 3, :]
    xn = x_ref[0] + coef * (gate * _rms(y_ref[0], gpost_ref[...]))
    xo_ref[0] = xn
    shift = modn_ref[3 * nsub:3 * nsub + 1, :]
    scale = modn_ref[3 * nsub + 1:3 * nsub + 2, :]
    h_ref[0] = (_rms(xn, gpre_ref[...]) * (1.0 + scale) + shift).astype(h_ref.dtype)


def _post_last_kernel(x_ref, y_ref, gpost_ref, mod_ref, xo_ref, *, sub, coef):
    gate = mod_ref[3 * sub + 2:3 * sub + 3, :]
    xo_ref[0] = x_ref[0] + coef * (gate * _rms(y_ref[0], gpost_ref[...]))


def post_residual(x, y, norm_post, norm_pre, mod, l, sub, coef, nxt, ts):
    B, S, D = x.shape
    xspec = pl.BlockSpec((1, ts, D), lambda b, i: (b, i, 0))
    gspec = lambda ll, ss: pl.BlockSpec((None, None, 1, D), lambda b, i: (ll, ss, 0, 0))
    mspec = lambda ll: pl.BlockSpec((None, None, 3 * N_SUB, D), lambda b, i: (ll, b, 0, 0))
    if nxt is None:
        return pl.pallas_call(
            functools.partial(_post_last_kernel, sub=sub, coef=coef),
            out_shape=jax.ShapeDtypeStruct((B, S, D), F32),
            grid=(B, S // ts),
            in_specs=[xspec, xspec, gspec(l, sub), mspec(l)],
            out_specs=xspec,
            compiler_params=_cparams(("parallel", "parallel")),
            name="post_last",
        )(x, y, norm_post, mod), None
    l2, sub2 = nxt
    return pl.pallas_call(
        functools.partial(_post_kernel, sub=sub, coef=coef, nsub=sub2),
        out_shape=(jax.ShapeDtypeStruct((B, S, D), F32), jax.ShapeDtypeStruct((B, S, D), BF16)),
        grid=(B, S // ts),
        in_specs=[xspec, xspec, gspec(l, sub), mspec(l), gspec(l2, sub2), mspec(l2)],
        out_specs=(xspec, xspec),
        compiler_params=_cparams(("parallel", "parallel")),
        name="post_residual",
    )(x, y, norm_post, mod, norm_pre, mod)


def _mm_kernel(a_ref, w_ref, o_ref, *, act):
    acc = jnp.dot(a_ref[...], w_ref[...], preferred_element_type=F32)
    if act == "sigmoid":
        acc = jax.nn.sigmoid(acc)
    o_ref[...] = acc.astype(o_ref.dtype)


def matmul(a, w, l, tm, tn, out_dtype, act=None, name="matmul", col0=0, n_cols=None):
    M, K = a.shape
    N = w.shape[-1] if n_cols is None else n_cols
    assert col0 % tn == 0 and N % tn == 0
    j0 = col0 // tn
    return pl.pallas_call(
        functools.partial(_mm_kernel, act=act),
        out_shape=jax.ShapeDtypeStruct((M, N), out_dtype),
        grid=(M // tm, N // tn),
        in_specs=[
            pl.BlockSpec((tm, K), lambda i, j: (i, 0)),
            pl.BlockSpec((None, K, tn), lambda i, j: (l, 0, j0 + j)),
        ],
        out_specs=pl.BlockSpec((tm, tn), lambda i, j: (i, j)),
        compiler_params=_cparams(("parallel", "arbitrary")),
        name=name,
    )(a, w)


def _swiglu_kernel(a_ref, wa_ref, wb_ref, o_ref):
    h = a_ref[...]
    a = jnp.dot(h, wa_ref[...], preferred_element_type=F32)
    b = jnp.dot(h, wb_ref[...], preferred_element_type=F32)
    o_ref[...] = (_silu(a) * b).astype(o_ref.dtype)


def swiglu_in(h, w, l, tm, tn):
    M, K = h.shape
    F = w.shape[-1] // 2
    nb = F // tn
    return pl.pallas_call(
        _swiglu_kernel,
        out_shape=jax.ShapeDtypeStruct((M, F), BF16),
        grid=(M // tm, nb),
        in_specs=[
            pl.BlockSpec((tm, K), lambda i, j: (i, 0)),
            pl.BlockSpec((None, K, tn), lambda i, j: (l, 0, j)),
            pl.BlockSpec((None, K, tn), lambda i, j: (l, 0, j + nb)),
        ],
        out_specs=pl.BlockSpec((tm, tn), lambda i, j: (i, j)),
        compiler_params=_cparams(("parallel", "arbitrary")),
        name="swiglu_in",
    )(h, w, w)


def _merge_kernel(o1_ref, o2_ref, o3_ref, w1_ref, w2_ref, w3_ref, g1_ref, g2_ref, g3_ref, o_ref):
    acc = g1_ref[...].astype(F32) * jnp.dot(o1_ref[...], w1_ref[...], preferred_element_type=F32)
    acc += g2_ref[...].astype(F32) * jnp.dot(o2_ref[...], w2_ref[...], preferred_element_type=F32)
    acc += g3_ref[...].astype(F32) * jnp.dot(o3_ref[...], w3_ref[...], preferred_element_type=F32)
    o_ref[...] = acc.astype(o_ref.dtype)


def branch_merge(o_ret, o_nsa, o_rwkv, w_ret, w_nsa, w_rwkv, zg, l, tm, tn):
    M = o_ret.shape[0]
    D = w_ret.shape[-1]
    nb = D // tn
    ospec = lambda o: pl.BlockSpec((tm, o.shape[1]), lambda i, j: (i, 0))
    wspec = lambda w: pl.BlockSpec((None, w.shape[1], tn), lambda i, j: (l, 0, j))
    gspec = lambda k: pl.BlockSpec((tm, tn), lambda i, j: (i, j + k * nb))
    return pl.pallas_call(
        _merge_kernel,
        out_shape=jax.ShapeDtypeStruct((M, D), BF16),
        grid=(M // tm, nb),
        in_specs=[ospec(o_ret), ospec(o_nsa), ospec(o_rwkv), wspec(w_ret), wspec(w_nsa), wspec(w_rwkv),
                  gspec(0), gspec(1), gspec(2)],
        out_specs=pl.BlockSpec((tm, tn), lambda i, j: (i, j)),
        compiler_params=_cparams(("parallel", "arbitrary")),
        name="branch_merge",
    )(o_ret, o_nsa, o_rwkv, w_ret, w_nsa, w_rwkv, zg, zg, zg)


def _ret_kernel(q_ref, k_ref, v_ref, g_ref, cos_ref, sin_ref, dm_ref, zeta_ref, xi_ref, o_ref, r_scr, *, decays):
    @pl.when(pl.program_id(1) == 0)
    def _():
        r_scr[...] = jnp.zeros_like(r_scr)

    cos = cos_ref[...]
    sin = sin_ref[...]
    dk, dv = RET_QK_HD, RET_V_HD
    for h in range(RET_HEADS):
        qh = q_ref[0, :, h * dk:(h + 1) * dk]
        kh = k_ref[0, :, h * dk:(h + 1) * dk]
        qh = qh * cos + pltpu.roll(qh, dk // 2, 1) * sin
        kh = (kh * cos + pltpu.roll(kh, dk // 2, 1) * sin) * (dk ** -0.5)
        vh = v_ref[0, :, h * dv:(h + 1) * dv]
        s = _dot_nt(qh, kh) * dm_ref[h]
        state = r_scr[h]
        o = _dot(s, vh) + _dot(qh, state) * xi_ref[h]
        mu = jnp.mean(o, axis=-1, keepdims=True)
        d = o - mu
        var = jnp.mean(d * d, axis=-1, keepdims=True)
        on = d * lax.rsqrt(var + 1e-6)
        gh = g_ref[0, :, h * dv:(h + 1) * dv]
        o_ref[0, :, h * dv:(h + 1) * dv] = (_silu(gh) * on).astype(o_ref.dtype)
        r_scr[h] = _dot_tn(kh * zeta_ref[h], vh) + decays[h] * state


def retention(z3):
    B, S, _ = z3.shape
    H, C = RET_HEADS, RET_CHUNK
    pos = jnp.arange(S, dtype=F32)
    inv = 1.0 / (ROPE_BASE ** jnp.linspace(0.0, 1.0, RET_QK_HD // 2))
    ang = pos[:, None] * inv[None, :]
    cos, sin = jnp.cos(ang), jnp.sin(ang)
    cos_f = jnp.concatenate([cos, cos], axis=-1)
    sin_f = jnp.concatenate([-sin, sin], axis=-1)
    log_g = jnp.log1p(-(2.0 ** (-5.0 - jnp.arange(H, dtype=F32))))
    idx = jnp.arange(C, dtype=F32)
    diff = idx[:, None] - idx[None, :]
    dmask = jnp.where(diff >= 0, jnp.exp(jnp.maximum(diff, 0.0)[None] * log_g[:, None, None]), 0.0)
    zeta = jnp.exp((C - 1 - idx)[None, :] * log_g[:, None])
    xi = jnp.exp((idx + 1)[None, :] * log_g[:, None])
    zeta_t = jnp.broadcast_to(zeta[:, :, None], (H, C, RET_QK_HD))
    xi_t = jnp.broadcast_to(xi[:, :, None], (H, C, RET_V_HD))
    lg64 = np.log1p(-(2.0 ** (-5.0 - np.arange(H, dtype=np.float64))))
    decays = tuple(float(v) for v in np.exp(C * lg64))
    qb = ZM_RET // RET_QK
    vb = (ZM_RET + 2 * RET_QK) // RET_V
    const = lambda shape: pl.BlockSpec(shape, lambda b, c: (0,) * len(shape))
    return pl.pallas_call(
        functools.partial(_ret_kernel, decays=decays),
        out_shape=jax.ShapeDtypeStruct((B, S, RET_V), BF16),
        grid=(B, S // C),
        in_specs=[
            pl.BlockSpec((1, C, RET_QK), lambda b, c: (b, c, qb)),
            pl.BlockSpec((1, C, RET_QK), lambda b, c: (b, c, qb + 1)),
            pl.BlockSpec((1, C, RET_V), lambda b, c: (b, c, vb)),
            pl.BlockSpec((1, C, RET_V), lambda b, c: (b, c, vb + 1)),
            pl.BlockSpec((C, RET_QK_HD), lambda b, c: (c, 0)),
            pl.BlockSpec((C, RET_QK_HD), lambda b, c: (c, 0)),
            const((H, C, C)), const((H, C, RET_QK_HD)), const((H, C, RET_V_HD)),
        ],
        out_specs=pl.BlockSpec((1, C, RET_V), lambda b, c: (b, c, 0)),
        scratch_shapes=[pltpu.VMEM((H, RET_QK_HD, RET_V_HD), F32)],
        compiler_params=_cparams(("parallel", "arbitrary")),
        name="retention",
    )(z3, z3, z3, z3, cos_f, sin_f, dmask, zeta_t, xi_t)


def _rel_bucket(dist):
    n = jnp.maximum(dist, 0)
    max_exact = REL_BUCKETS // 2
    nf = jnp.maximum(n, 1).astype(F32)
    large = max_exact + (jnp.log(nf / max_exact) / math.log(REL_MAX_DIST / max_exact)
                         * (REL_BUCKETS - max_exact)).astype(jnp.int32)
    large = jnp.minimum(large, REL_BUCKETS - 1)
    return jnp.where(n < max_exact, n, large)


def _cmp_kernel(kc_ref, vc_ref, pos_ref, w1_ref, b1_ref, w2_ref, b2_ref, ko_ref, vo_ref, *, nb):
    d = NSA_HD
    half = CMP_BLOCK // 2
    for i, (src, dst) in enumerate(((kc_ref, ko_ref), (vc_ref, vo_ref))):
        p1 = jnp.zeros((nb, w1_ref.shape[-1]), F32)
        p2 = jnp.zeros((nb, w1_ref.shape[-1]), F32)
        for t in range(half):
            a = src[0, pl.ds(t, nb, stride=CMP_STRIDE), :]
            p1 += _dot(a + pos_ref[i, t:t + 1, :], w1_ref[i, t * d:(t + 1) * d, :])
            p2 += _dot(a + pos_ref[i, half + t:half + t + 1, :], w1_ref[i, (half + t) * d:(half + t + 1) * d, :])
        pre = p1 + pltpu.roll(p2, nb - 1, 0) + b1_ref[i]
        dst[0, 0] = _dot(jax.nn.gelu(pre), w2_ref[i]) + b2_ref[i]


def nsa_compress(z3, cmp_pos, cmp_w1, cmp_b1, cmp_w2, cmp_b2):
    B, S, _ = z3.shape
    G, d = NSA_KV_HEADS, NSA_HD
    nb = S // CMP_STRIDE
    kb = ZM_NSA_KV // d
    hid = cmp_w1.shape[-1]
    const = lambda shape: pl.BlockSpec(shape, lambda b, g: (0,) * len(shape))
    out = jax.ShapeDtypeStruct((B, G, nb, d), F32)
    return pl.pallas_call(
        functools.partial(_cmp_kernel, nb=nb),
        out_shape=(out, out),
        grid=(B, G),
        in_specs=[
            pl.BlockSpec((1, S, d), lambda b, g: (b, 0, kb + g)),
            pl.BlockSpec((1, S, d), lambda b, g: (b, 0, kb + G + g)),
            const((2, CMP_BLOCK, d)), const((2, CMP_BLOCK * d, hid)), const((2, 1, hid)),
            const((2, hid, d)), const((2, 1, d)),
        ],
        out_specs=(pl.BlockSpec((1, 1, nb, d), lambda b, g: (b, g, 0, 0)),
                   pl.BlockSpec((1, 1, nb, d), lambda b, g: (b, g, 0, 0))),
        compiler_params=_cparams(("parallel", "parallel")),
        name="nsa_compress",
    )(z3, z3, cmp_pos, cmp_w1, cmp_b1.reshape(2, 1, hid), cmp_w2, cmp_b2.reshape(2, 1, d))


def _nsa_kernel(q_ref, gate_ref, kc_ref, vc_ref, ks_ref, vs_ref, kw_ref, vw_ref, bc_ref, bs_ref, bw_ref, ovt_ref, e_ref,
                o_ref, q_scr, os_scr, ksf, vsf, kwf, vwf, *, n_sblk, nkb):
    g = pl.program_id(1)
    qi = pl.program_id(2)
    T = LANES
    HG = NSA_GROUP
    nwb = WINDOW // T + 1
    spb = T // SEL_BLOCK

    @pl.when(qi == 0)
    def _():
        for src, dst in ((ks_ref, ksf), (vs_ref, vsf), (kw_ref, kwf), (vw_ref, vwf)):
            for blk in range(nkb):
                dst[(nkb - 1 - blk) * T:(nkb - blk) * T, :] = src[0, blk * T:(blk + 1) * T, :].astype(BF16)
            dst[nkb * T:, :] = jnp.zeros(((nkb - 1) * T, NSA_HD), BF16)

    for j in range(HG):
        q_scr[j * T:(j + 1) * T, :] = (q_ref[0, :, j * T:(j + 1) * T] * (NSA_HD ** -0.5)).astype(BF16)
    q_all = q_scr[...]
    start = pl.multiple_of((nkb - 1 - qi) * T, T)

    def softmax_pv(s, v):
        m = jnp.max(s, axis=-1, keepdims=True)
        p = jnp.exp(s - m)
        den = jnp.sum(p, axis=-1, keepdims=True)
        o = jnp.dot(p.reshape(HG * T, p.shape[-1]).astype(BF16), v, preferred_element_type=F32)
        return o.reshape(HG, T, v.shape[-1]) / den

    s = _dot_nt(q_all, kc_ref[0, 0]).reshape(HG, T, T) + bc_ref[...]
    m = jnp.max(s, axis=-1, keepdims=True)
    p = jnp.where(s > 0.5 * NEG, jnp.exp(s - m), 0.0)
    den = jnp.sum(p, axis=-1, keepdims=True)
    pn = p / jnp.where(den > 0.0, den, 1.0)
    o_c = _dot(pn.reshape(HG * T, T), vc_ref[0, 0]).reshape(HG, T, NSA_HD)
    psum = pn[0]
    for j in range(1, HG):
        psum = psum + pn[j]

    p_hi = psum.astype(BF16)
    p_lo = (psum - p_hi.astype(F32)).astype(BF16)
    imp = (_dot_nt(ovt_ref[...], p_hi) + _dot_nt(ovt_ref[...], p_lo))[0:n_sblk, :]
    blk = lax.broadcasted_iota(jnp.int32, (n_sblk, T), 0)
    lane = lax.broadcasted_iota(jnp.int32, (n_sblk, T), 1)
    cur = (qi * T + lane) // SEL_BLOCK
    forced = (blk == 0) | (blk == cur) | (blk == cur - 1)
    impm = jnp.where(forced, 1e30, jnp.where(blk <= cur, imp, NEG))
    rank = jnp.zeros((n_sblk, T), F32)
    for mb in range(n_sblk):
        cm = impm[mb:mb + 1, :]
        lower = jnp.where(blk > mb, 1.0, 0.0)
        rank += jnp.where(cm > impm, 1.0, jnp.where(cm == impm, lower, 0.0))
    sel_t = jnp.where(rank < float(min(N_SEL, n_sblk)), 1.0, 0.0)

    rev = jnp.where(blk + lane == spb * qi + spb - 1, 1.0, 0.0)
    sel_back = _dot_tn(sel_t, rev).astype(BF16)

    def selected(ncol):
        keep = jnp.dot(sel_back, e_ref[:, 0:ncol], preferred_element_type=F32)
        s = _dot_nt(q_all, ksf[pl.ds(start, ncol), :]).reshape(HG, T, ncol) + bs_ref[:, :, 0:ncol]
        os_scr[...] = softmax_pv(s + ((keep - 1.0) * -NEG)[None], vsf[pl.ds(start, ncol), :])

    quarter = max(nkb // 4, 1)
    bounds = list(range(quarter, nkb, quarter)) + [nkb]
    for lo_blk, hi_blk in zip([0] + bounds[:-1], bounds):
        pl.when((qi >= lo_blk) & (qi < hi_blk))(functools.partial(selected, hi_blk * T))
    o_s = os_scr[...]

    colw = lax.broadcasted_iota(jnp.int32, (T, nwb * T), 1)
    s = _dot_nt(q_all, kwf[pl.ds(start, nwb * T), :]).reshape(HG, T, nwb * T) + bw_ref[...]
    o_w = softmax_pv(s + jnp.where(colw < (qi + 1) * T, 0.0, NEG)[None], vwf[pl.ds(start, nwb * T), :])

    gates = jax.nn.sigmoid(gate_ref[0])
    for j in range(HG):
        base = 3 * j
        gsel = lambda c: jnp.where(g == 0, gates[:, c:c + 1], gates[:, 3 * HG + c:3 * HG + c + 1])
        o = gsel(base) * o_c[j] + gsel(base + 1) * o_s[j] + gsel(base + 2) * o_w[j]
        o_ref[0, :, j * T:(j + 1) * T] = o.astype(o_ref.dtype)


def nsa_tables(rel_bias, S):
    T = LANES
    n_cmp = (S - CMP_BLOCK) // CMP_STRIDE + 1
    n_sblk = S // SEL_BLOCK
    nkb = S // T
    nwb = WINDOW // T + 1
    assert n_cmp < T and n_sblk <= T and S // CMP_STRIDE == T and nwb <= nkb
    H = rel_bias.shape[1]
    tab = rel_bias[_rel_bucket(jnp.arange(S + T)), :].T
    ii = np.arange(T)[:, None]
    cc = np.arange(nkb * T)[None, :]
    dist = ii + T * (cc // T) - cc % T
    tab_p = jnp.pad(tab, ((0, 0), (T - 1, 0)))
    hank = jnp.stack([tab_p[:, i:i + nkb * T] for i in range(T)], axis=1)
    bias = hank.reshape(H, T, nkb, T)[..., ::-1].reshape(H, T, nkb * T)
    bias_s = jnp.where(dist >= 0, bias, NEG)
    bias_w = jnp.where((dist >= 0) & (dist < WINDOW), bias, NEG)[:, :, :nwb * T]
    ends = CMP_STRIDE * np.arange(T) + CMP_BLOCK - 1
    valid_c = ((np.arange(S)[:, None] - ends[None, :]) >= 0) & (np.arange(T)[None, :] < n_cmp)
    front = int(ends[-1])
    tab_q = jnp.pad(tab, ((0, 0), (front, 0)))
    cols = jnp.stack([tab_q[:, front - int(e):front - int(e) + S] for e in ends], axis=-1)
    bias_c = jnp.where(valid_c, cols, NEG)

    cmp_start = CMP_STRIDE * np.arange(T)
    sel_start = SEL_BLOCK * np.arange(T)
    ov = ((cmp_start[:, None] <= (sel_start + SEL_BLOCK - 1)[None, :])
          & ((cmp_start + CMP_BLOCK - 1)[:, None] >= sel_start[None, :])
          & (np.arange(T)[:, None] < n_cmp) & (np.arange(T)[None, :] < n_sblk))
    spb = T // SEL_BLOCK
    c = np.arange(nkb * T)
    back = spb * (c // T) + spb - 1 - (c % T) // SEL_BLOCK
    expand = np.arange(T)[:, None] == back[None, :]
    return bias_c, bias_s, bias_w, jnp.asarray(ov.T, BF16), jnp.asarray(expand, BF16)


def nsa_attention(zm, zt, kcmp, vcmp, tables):
    B, S, _ = zm.shape
    G, HG, d, T = NSA_KV_HEADS, NSA_GROUP, NSA_HD, LANES
    assert G == 2 and d == T
    n_sblk = S // SEL_BLOCK
    nkb = S // T
    nwb = WINDOW // T + 1
    bias_c, bias_s, bias_w, ov_t, expand = tables

    qb = ZM_NSA_Q // (HG * d)
    kvb = ZM_NSA_KV // d
    kvspec = lambda off: pl.BlockSpec((1, S, d), lambda b, g, i: (b, 0, kvb + off * G + g))
    cspec = pl.BlockSpec((1, 1, T, d), lambda b, g, i: (b, g, 0, 0))
    flip = pltpu.VMEM(((2 * nkb - 1) * T, d), BF16)
    return pl.pallas_call(
        functools.partial(_nsa_kernel, n_sblk=n_sblk, nkb=nkb),
        out_shape=jax.ShapeDtypeStruct((B, S, NSA_Q), BF16),
        grid=(B, G, nkb),
        in_specs=[
            pl.BlockSpec((1, T, HG * d), lambda b, g, i: (b, i, qb + g)),
            pl.BlockSpec((1, T, NSA_GATE_PAD), lambda b, g, i: (b, i, ZT_GATE // NSA_GATE_PAD)),
            cspec, cspec, kvspec(2), kvspec(3), kvspec(4), kvspec(5),
            pl.BlockSpec((HG, T, T), lambda b, g, i: (g, i, 0)),
            pl.BlockSpec((HG, T, nkb * T), lambda b, g, i: (g, 0, 0)),
            pl.BlockSpec((HG, T, nwb * T), lambda b, g, i: (g, 0, 0)),
            pl.BlockSpec((T, T), lambda b, g, i: (0, 0)),
            pl.BlockSpec((T, nkb * T), lambda b, g, i: (0, 0)),
        ],
        out_specs=pl.BlockSpec((1, T, HG * d), lambda b, g, i: (b, i, g)),
        scratch_shapes=[pltpu.VMEM((HG * T, d), BF16), pltpu.VMEM((HG, T, d), F32), flip, flip, flip, flip],
        compiler_params=_cparams(("parallel", "parallel", "arbitrary")),
        name="nsa_attention",
    )(zm, zt, kcmp, vcmp, zm, zm, zm, zm, bias_c, bias_s, bias_w, ov_t, expand)


def _rwkv_prep_kernel(zr_ref, zk_ref, zv_ref, zl_ref, mu_ref, mul_ref, vec_ref, w2_ref, a2_ref, g2_ref,
                      r_ref, k_ref, v_ref, lw_ref, kk_ref, kka_ref, gg_ref, c_scr, cl_scr):
    first = pl.program_id(1) == 0
    tb = zr_ref.shape[1]
    row = lax.broadcasted_iota(jnp.int32, (tb, 1), 0)

    def shift(z, mu, carry_ref, slot):
        prev_last = jnp.where(first, 0.0, carry_ref[slot:slot + 1, :])
        prev = jnp.where(row == 0, prev_last, pltpu.roll(z, 1, 0))
        carry_ref[slot:slot + 1, :] = z[tb - 1:tb, :]
        return z + (prev - z) * mu

    r = shift(zr_ref[0], mu_ref[0:1, :], c_scr, 0)
    k = shift(zk_ref[0], mu_ref[1:2, :], c_scr, 1)
    v = shift(zv_ref[0], mu_ref[2:3, :], c_scr, 2)
    xl = shift(zl_ref[0], mul_ref[...], cl_scr, 0)
    w0, a0, k_k, k_a = vec_ref[0:1, :], vec_ref[1:2, :], vec_ref[2:3, :], vec_ref[3:4, :]
    u = -(w0 + _dot(jnp.tanh(xl), w2_ref[...]))
    w_log = -(jnp.maximum(u, 0.0) + jnp.log1p(jnp.exp(-jnp.abs(u)))) - 0.5
    a = jax.nn.sigmoid(a0 + _dot(xl, a2_ref[...]))
    r_ref[0] = r
    k_ref[0] = k * (1.0 + (a - 1.0) * k_a)
    v_ref[0] = v
    lw_ref[0] = -jnp.exp(w_log)
    kk = k * k_k
    kk_ref[0] = kk
    kka_ref[0] = kk * a
    gg_ref[0] = _dot(jax.nn.sigmoid(xl), g2_ref[...])


def rwkv_prep(z3, mu_rkv, mu_lora, vecs, w2p, a2p, g2p, tb):
    B, S, _ = z3.shape
    Dm = RWKV_DIM
    rb = ZT_RWKV // Dm
    zspec = lambda o: pl.BlockSpec((1, tb, Dm), lambda b, i: (b, i, rb + o))
    const = lambda shape: pl.BlockSpec(shape, lambda b, i: (0,) * len(shape))
    ospec = pl.BlockSpec((1, tb, Dm), lambda b, i: (b, i, 0))
    out = jax.ShapeDtypeStruct((B, S, Dm), F32)
    return pl.pallas_call(
        _rwkv_prep_kernel,
        out_shape=(out,) * 7,
        grid=(B, S // tb),
        in_specs=[zspec(0), zspec(1), zspec(2),
                  pl.BlockSpec((1, tb, LORA_PAD), lambda b, i: (b, i, ZT_LORA // LORA_PAD)),
                  const((3, Dm)), const((1, LORA_PAD)), const((4, Dm)),
                  const((LORA_PAD, Dm)), const((LORA_PAD, Dm)), const((LORA_PAD, Dm))],
        out_specs=(ospec,) * 7,
        scratch_shapes=[pltpu.VMEM((8, Dm), F32), pltpu.VMEM((8, LORA_PAD), F32)],
        compiler_params=_cparams(("parallel", "arbitrary")),
        name="rwkv_prep",
    )(z3, z3, z3, z3, mu_rkv, mu_lora, vecs, w2p, a2p, g2p)


def _rwkv_masks():
    T, n = RWKV_CHUNK, RWKV_GW
    idx = np.arange(n)
    h, t = idx // T, idx % T
    same = h[:, None] == h[None, :]
    tt, ss = t[:, None], t[None, :]
    levels = []
    b = 1
    while b < T:
        levels.append(same & (tt // (2 * b) == ss // (2 * b)) & (tt % (2 * b) >= b) & (ss % (2 * b) < b))
        b *= 2
    masks = [same, same & (ss < tt), same & (ss <= tt), np.eye(n, dtype=bool)] + levels
    return np.stack(masks).astype(np.float32), len(levels)


def _rwkv_chunk_kernel(r_ref, k_ref, v_ref, lw_ref, kk_ref, kka_ref, gg_ref, rk_ref, ln_ref, msk_ref, tri_ref,
                       o_ref, s_scr, *, n_levels):
    T, GW = RWKV_CHUNK, RWKV_GW

    @pl.when(pl.program_id(1) == 0)
    def _():
        s_scr[...] = jnp.zeros_like(s_scr)

    m_bd = msk_ref[0]
    m_strict = msk_ref[1]
    m_incl = msk_ref[2]
    eye = msk_ref[3].astype(F32)
    tri = tri_ref[...]

    def to_bd(x):
        return jnp.concatenate([x.astype(BF16)] * RWKV_GROUP, axis=0) * m_bd

    def from_bd(y):
        out = y[0:T]
        for i in range(1, RWKV_GROUP):
            out = out + y[i * T:(i + 1) * T]
        return out

    def seg_sum(x):
        return _split_dot(x, m_bd, 2)

    def mm(a, b):
        return jnp.dot(a, b, preferred_element_type=F32)

    def mm_nt(a, b):
        return lax.dot_general(a, b, (((1,), (1,)), ((), ())), preferred_element_type=F32)

    def mm_tn(a, b):
        return lax.dot_general(a, b, (((0,), (0,)), ((), ())), preferred_element_type=F32)

    groups = range(RWKV_DIM // GW)
    sls = [slice(gi * GW, (gi + 1) * GW) for gi in groups]
    rs = [r_ref[0, :, sl] for sl in sls]
    ks = [k_ref[0, :, sl] for sl in sls]
    vs = [v_ref[0, :, sl] for sl in sls]
    lws = [lw_ref[0, :, sl] for sl in sls]
    inv_n = [1.0 / jnp.maximum(jnp.sqrt(seg_sum(kk_ref[0, :, sl] * kk_ref[0, :, sl])), 1e-12) for sl in sls]
    cum = [_split_dot_left(tri, lw) for lw in lws]
    p_in = [jnp.exp(c) for c in cum]
    p_inv = [jnp.exp(-c) for c in cum]
    a_bd = [to_bd(-(kk_ref[0, :, sl] * n) * jnp.exp(c - lw)) for sl, n, c, lw in zip(sls, inv_n, cum, lws)]
    r_bd = [to_bd(r * p) for r, p in zip(rs, p_in)]
    b_bd = [to_bd(kka_ref[0, :, sl] * n * p) for sl, n, p in zip(sls, inv_n, p_inv)]
    k_bd = [to_bd(k * p) for k, p in zip(ks, p_inv)]
    v_bd = [to_bd(v) for v in vs]

    a_ab = [(mm_nt(a, b) * m_strict).astype(BF16) for a, b in zip(a_bd, b_bd)]
    a_ak = [(mm_nt(a, k) * m_strict).astype(BF16) for a, k in zip(a_bd, k_bd)]
    a_rb = [(mm_nt(r, b) * m_incl).astype(BF16) for r, b in zip(r_bd, b_bd)]
    a_rk = [(mm_nt(r, k) * m_incl).astype(BF16) for r, k in zip(r_bd, k_bd)]

    x = [eye + a * msk_ref[4] for a in a_ab]
    for lv in range(1, n_levels):
        xb = [xi.astype(BF16) for xi in x]
        t = [mm(xi, a * msk_ref[4 + lv]).astype(BF16) for xi, a in zip(xb, a_ab)]
        x = [xf + mm(ti, xi) for xf, ti, xi in zip(x, t, xb)]
    xb = [xi.astype(BF16) for xi in x]

    state = [s_scr[gi] for gi in groups]
    sb = [s.astype(BF16) for s in state]
    rhs = [(mm_nt(a, s) + mm(ak, v)).astype(BF16) for a, s, ak, v in zip(a_bd, sb, a_ak, v_bd)]
    u = [mm(xi, q).astype(BF16) for xi, q in zip(xb, rhs)]
    y = [from_bd(mm_nt(r, s) + mm(arb, ui) + mm(ark, v))
         for r, s, arb, ui, ark, v in zip(r_bd, sb, a_rb, u, a_rk, v_bd)]
    for gi in groups:
        s_scr[gi] = (state[gi] + mm_tn(u[gi], b_bd[gi]) + mm_tn(v_bd[gi], k_bd[gi])) * p_in[gi][T - 1:T, :]

    mu = [seg_sum(yi) * (1.0 / RWKV_HD) for yi in y]
    d = [yi - m for yi, m in zip(y, mu)]
    var = [seg_sum(di * di) * (1.0 / RWKV_HD) for di in d]
    bonus = [seg_sum(r * k * rk_ref[:, sl]) * v for r, k, v, sl in zip(rs, ks, vs, sls)]
    for gi, sl in enumerate(sls):
        yn = d[gi] * lax.rsqrt(var[gi] + RWKV_LN_EPS) * ln_ref[0:1, sl] + ln_ref[1:2, sl]
        o_ref[0, :, sl] = ((yn + bonus[gi]) * gg_ref[0, :, sl]).astype(o_ref.dtype)


def _split_dot_left(w_bf16, x):
    acc = None
    rem = x
    for _ in range(3):
        hi = rem.astype(BF16)
        t = jnp.dot(w_bf16, hi, preferred_element_type=F32)
        acc = t if acc is None else acc + t
        rem = rem - hi.astype(F32)
    return acc


def rwkv_chunk(r, k, v, lw, kk, kka, gg, rk, ln):
    B, S, Dm = r.shape
    T = RWKV_CHUNK
    masks, n_levels = _rwkv_masks()
    tri = np.tril(np.ones((T, T), np.float32))
    ng = Dm // RWKV_GW
    xspec = pl.BlockSpec((1, T, Dm), lambda b, c: (b, c, 0))
    const = lambda shape: pl.BlockSpec(shape, lambda b, c: (0,) * len(shape))
    return pl.pallas_call(
        functools.partial(_rwkv_chunk_kernel, n_levels=n_levels),
        out_shape=jax.ShapeDtypeStruct((B, S, Dm), BF16),
        grid=(B, S // T),
        in_specs=[xspec] * 7 + [const((1, Dm)), const((2, Dm)), const(masks.shape), const((T, T))],
        out_specs=xspec,
        scratch_shapes=[pltpu.VMEM((ng, RWKV_GW, RWKV_GW), F32)],
        compiler_params=_cparams(("parallel", "arbitrary")),
        name="rwkv_chunk",
    )(r, k, v, lw, kk, kka, gg, rk, ln, jnp.asarray(masks, BF16), jnp.asarray(tri, BF16))


def _cast_kernel(x_ref, o_ref):
    o_ref[...] = x_ref[...].astype(o_ref.dtype)


def cast_leading_cols(w, n_cols, tr, tc):
    L, K, _ = w.shape
    spec = pl.BlockSpec((None, tr, tc), lambda l, i, j: (l, i, j))
    return pl.pallas_call(
        _cast_kernel,
        out_shape=jax.ShapeDtypeStruct((L, K, n_cols), BF16),
        grid=(L, K // tr, n_cols // tc),
        in_specs=[spec],
        out_specs=spec,
        compiler_params=_cparams(("parallel", "parallel", "parallel")),
        name="cast_w_main",
    )(w)


def _prep_weights(w_in, rwkv_mu, rwkv_w2, rwkv_a2, rwkv_g2):
    L, D, _ = w_in.shape
    o = 3 * D + ZM_COLS
    w_main = cast_leading_cols(w_in, o, min(2048, D), 512)
    nsa_gate = w_in[:, :, o:o + 3 * NSA_HEADS]
    o += 3 * NSA_HEADS
    rkv = w_in[:, :, o:o + 3 * RWKV_DIM]
    lora = w_in[:, :, o + 3 * RWKV_DIM:o + RWKV_COLS]
    pad = lambda w, n: jnp.pad(w, ((0, 0), (0, 0), (0, n - w.shape[-1])))
    w_tail = jnp.concatenate([rkv, pad(lora, LORA_PAD), pad(nsa_gate, NSA_GATE_PAD)], axis=-1).astype(BF16)
    mu_rkv = rwkv_mu[:, :3 * RWKV_DIM].reshape(L, 3, RWKV_DIM)
    mu_lora = jnp.pad(rwkv_mu[:, 3 * RWKV_DIM:], ((0, 0), (0, LORA_PAD - LORA_COLS))).reshape(L, 1, LORA_PAD)
    rows = lambda w, start: jnp.pad(w, ((0, 0), (start, LORA_PAD - start - w.shape[1]), (0, 0)))
    w2p = rows(rwkv_w2, 0)
    a2p = rows(rwkv_a2, DECAY_LORA)
    g2p = rows(rwkv_g2, DECAY_LORA + AAA_LORA)
    return w_main, w_tail, mu_rkv, mu_lora, w2p, a2p, g2p


def kernel(x, c, rel_bias, w_in, w_branch_ret, w_branch_nsa, w_branch_rwkv, w_out, ffn1_in, ffn1_out, ffn2_in,
           ffn2_out, ada_down, ada_up, ada_bias, norm_pre, norm_post, cmp_pos, cmp_w1, cmp_b1, cmp_w2, cmp_b2,
           rwkv_mu, rwkv_vecs, rwkv_w2, rwkv_a2, rwkv_g2, rwkv_rk, rwkv_ln):
    B, S, D = x.shape
    L = w_in.shape[0]
    M = B * S
    tm = min(1024, M)
    ts = min(256, S)
    tb = min(256, S)
    tn_gate = _col_tile(3 * D, 1024)
    tn_mix = _col_tile(math.gcd(3 * D, ZM_COLS), 1024)
    tn_tail = _col_tile(ZT_COLS, 1024)
    tn_d = _col_tile(D, 1024)

    w_main, w_tail, mu_rkv, mu_lora, w2p, a2p, g2p = _prep_weights(w_in, rwkv_mu, rwkv_w2, rwkv_a2, rwkv_g2)
    tables = nsa_tables(rel_bias, S)
    w_bret, w_bnsa, w_brwkv = w_branch_ret.astype(BF16), w_branch_nsa.astype(BF16), w_branch_rwkv.astype(BF16)
    w_o = w_out.astype(BF16)
    f1i, f1o, f2i, f2o = (w.astype(BF16) for w in (ffn1_in, ffn1_out, ffn2_in, ffn2_out))
    npre = norm_pre.reshape(L, N_SUB, 1, D)
    npost = norm_post.reshape(L, N_SUB, 1, D)
    rk = rwkv_rk.reshape(L, 1, RWKV_DIM)

    mod = ada_mod(c, ada_down, ada_up, ada_bias)
    h = norm_modulate(x, npre, mod, 0, 0, ts)
    for l in range(L):
        u = swiglu_in(h.reshape(M, D), f1i, l, tm, 512)
        y = matmul(u, f1o, l, tm, tn_d, F32, name="ffn_out")
        x, h = post_residual(x, y.reshape(B, S, D), npost, npre, mod, l, 0, 0.5, (l, 1), ts)
        h2 = h.reshape(M, D)
        zg = matmul(h2, w_main, l, tm, tn_gate, BF16, act="sigmoid", name="gate_proj", n_cols=3 * D)
        zm = matmul(h2, w_main, l, tm, tn_mix, F32, name="mix_proj", col0=3 * D, n_cols=ZM_COLS)
        zm = zm.reshape(B, S, ZM_COLS)
        zt = matmul(h2, w_tail, l, tm, tn_tail, F32, name="tail_proj").reshape(B, S, ZT_COLS)
        o_ret = retention(zm)
        kcmp, vcmp = nsa_compress(zm, cmp_pos[l], cmp_w1[l], cmp_b1[l], cmp_w2[l], cmp_b2[l])
        o_nsa = nsa_attention(zm, zt, kcmp, vcmp, tables)
        rw = rwkv_prep(zt, mu_rkv[l], mu_lora[l], rwkv_vecs[l], w2p[l], a2p[l], g2p[l], tb)
        o_rwkv = rwkv_chunk(*rw, rk[l], rwkv_ln[l])
        merged = branch_merge(o_ret.reshape(M, RET_V), o_nsa.reshape(M, NSA_Q), o_rwkv.reshape(M, RWKV_DIM),
                              w_bret, w_bnsa, w_brwkv, zg, l, tm, 512)
        y = matmul(merged, w_o, l, tm, tn_d, F32, name="out_proj")
        x, h = post_residual(x, y.reshape(B, S, D), npost, npre, mod, l, 1, 1.0, (l, 2), ts)
        u = swiglu_in(h.reshape(M, D), f2i, l, tm, 512)
        y = matmul(u, f2o, l, tm, tn_d, F32, name="ffn_out")
        nxt = (l + 1, 0) if l + 1 < L else None
        x, h = post_residual(x, y.reshape(B, S, D), npost, npre, mod, l, 2, 0.5, nxt, ts)
    return x
```

```python
import functools
import math

import numpy as np
import jax
import jax.numpy as jnp
from jax import lax
from jax.experimental import pallas as pl
from jax.experimental.pallas import tpu as pltpu

F32 = jnp.float32
BF16 = jnp.bfloat16

D_MODEL = 4096
DEPTH = 4
D_FF = 3072
N_SUB = 3
RET_HEADS, RET_QK_HD, RET_V_HD, RET_CHUNK = 8, 128, 256, 128
RET_QK = RET_HEADS * RET_QK_HD
RET_V = RET_HEADS * RET_V_HD
ROPE_BASE = 10000.0
NSA_HEADS, NSA_KV_HEADS, NSA_HD = 8, 2, 128
NSA_GROUP = NSA_HEADS // NSA_KV_HEADS
NSA_Q = NSA_HEADS * NSA_HD
NSA_KV = NSA_KV_HEADS * NSA_HD
CMP_BLOCK, CMP_STRIDE = 32, 16
SEL_BLOCK, N_SEL = 64, 8
WINDOW = 512
RWKV_HD, RWKV_DIM = 64, 1024
RWKV_HEADS = RWKV_DIM // RWKV_HD
DECAY_LORA, AAA_LORA, GATE_LORA = 64, 64, 160
RWKV_LN_EPS = 64e-5
REL_BUCKETS, REL_MAX_DIST = 32, 128
RET_COLS = 2 * RET_QK + 2 * RET_V
NSA_COLS = NSA_Q + 6 * NSA_KV + 3 * NSA_HEADS
RWKV_COLS = 3 * RWKV_DIM + DECAY_LORA + AAA_LORA + GATE_LORA
NEG = -1e30

LANES = 128
VMEM_LIMIT = 56 * 1024 * 1024

ZM_RET = 0
ZM_NSA_Q = RET_COLS
ZM_NSA_KV = ZM_NSA_Q + NSA_Q
ZM_COLS = ZM_NSA_KV + 6 * NSA_KV
ZT_RWKV = 0
ZT_LORA = 3 * RWKV_DIM
LORA_COLS = DECAY_LORA + AAA_LORA + GATE_LORA
LORA_PAD = 3 * LANES
ZT_GATE = ZT_LORA + LORA_PAD
NSA_GATE_PAD = LANES
ZT_COLS = ZT_GATE + NSA_GATE_PAD

RWKV_CHUNK = 64
RWKV_GROUP = 4
RWKV_GW = RWKV_GROUP * RWKV_HD


def _col_tile(n, pref):
    return max(t for t in range(LANES, min(n, pref) + 1, LANES) if n % t == 0)


def _cparams(sem):
    return pltpu.CompilerParams(dimension_semantics=sem, vmem_limit_bytes=VMEM_LIMIT)


def _dot(a, b):
    return jnp.dot(a.astype(BF16), b.astype(BF16), preferred_element_type=F32)


def _dot_nt(a, b):
    return lax.dot_general(a.astype(BF16), b.astype(BF16), (((1,), (1,)), ((), ())),
                           preferred_element_type=F32)


def _dot_tn(a, b):
    return lax.dot_general(a.astype(BF16), b.astype(BF16), (((0,), (0,)), ((), ())),
                           preferred_element_type=F32)


def _split_dot(x, w_bf16, parts):
    acc = None
    rem = x
    for _ in range(parts):
        hi = rem.astype(BF16)
        t = jnp.dot(hi, w_bf16, preferred_element_type=F32)
        acc = t if acc is None else acc + t
        rem = rem - hi.astype(F32)
    return acc


def _silu(x):
    return x * jax.nn.sigmoid(x)


def _ada_kernel(c_ref, down_ref, up_ref, bias_ref, o_ref):
    t = _dot(_silu(c_ref[...]), down_ref[...])
    o_ref[...] = _dot(t, up_ref[...]) + bias_ref[...]


def ada_mod(c, ada_down, ada_up, ada_bias):
    L, D, R = ada_down.shape
    B = c.shape[0]
    N = ada_up.shape[-1]
    tn = D
    out = pl.pallas_call(
        _ada_kernel,
        out_shape=jax.ShapeDtypeStruct((L, B, N), F32),
        grid=(L, N // tn),
        in_specs=[
            pl.BlockSpec((B, D), lambda l, j: (0, 0)),
            pl.BlockSpec((None, D, R), lambda l, j: (l, 0, 0)),
            pl.BlockSpec((None, R, tn), lambda l, j: (l, 0, j)),
            pl.BlockSpec((None, 1, tn), lambda l, j: (l, 0, j)),
        ],
        out_specs=pl.BlockSpec((None, B, tn), lambda l, j: (l, 0, j)),
        compiler_params=_cparams(("parallel", "parallel")),
        name="ada_mod",
    )(c, ada_down, ada_up, ada_bias.reshape(L, 1, N))
    return out.reshape(L, B, 3 * N_SUB, D)


def _rms(x, g):
    return x * lax.rsqrt(jnp.mean(x * x, axis=-1, keepdims=True) + 1e-6) * g


def _normmod_kernel(x_ref, g_ref, mod_ref, h_ref, *, sub):
    shift = mod_ref[3 * sub:3 * sub + 1, :]
    scale = mod_ref[3 * sub + 1:3 * sub + 2, :]
    h_ref[0] = (_rms(x_ref[0], g_ref[...]) * (1.0 + scale) + shift).astype(h_ref.dtype)


def norm_modulate(x, gain, mod, l, sub, ts):
    B, S, D = x.shape
    return pl.pallas_call(
        functools.partial(_normmod_kernel, sub=sub),
        out_shape=jax.ShapeDtypeStruct((B, S, D), BF16),
        grid=(B, S // ts),
        in_specs=[
            pl.BlockSpec((1, ts, D), lambda b, i: (b, i, 0)),
            pl.BlockSpec((None, None, 1, D), lambda b, i: (l, sub, 0, 0)),
            pl.BlockSpec((None, None, 3 * N_SUB, D), lambda b, i: (l, b, 0, 0)),
        ],
        out_specs=pl.BlockSpec((1, ts, D), lambda b, i: (b, i, 0)),
        compiler_params=_cparams(("parallel", "parallel")),
        name="norm_modulate",
    )(x, gain, mod)


def _post_kernel(x_ref, y_ref, gpost_ref, mod_ref, gpre_ref, modn_ref, xo_ref, h_ref, *, sub, coef, nsub):
    gate = mod_ref[3 * sub + 2:3 * sub + 3, :]
    xn = x_ref[0] + coef * (gate * _rms(y_ref[0], gpost_ref[...]))
    xo_ref[0] = xn
    shift = modn_ref[3 * nsub:3 * nsub + 1, :]
    scale = modn_ref[3 * nsub + 1:3 * nsub + 2, :]
    h_ref[0] = (_rms(xn, gpre_ref[...]) * (1.0 + scale) + shift).astype(h_ref.dtype)


def _post_last_kernel(x_ref, y_ref, gpost_ref, mod_ref, xo_ref, *, sub, coef):
    gate = mod_ref[3 * sub + 2:3 * sub + 3, :]
    xo_ref[0] = x_ref[0] + coef * (gate * _rms(y_ref[0], gpost_ref[...]))


def post_residual(x, y, norm_post, norm_pre, mod, l, sub, coef, nxt, ts):
    B, S, D = x.shape
    xspec = pl.BlockSpec((1, ts, D), lambda b, i: (b, i, 0))
    gspec = lambda ll, ss: pl.BlockSpec((None, None, 1, D), lambda b, i: (ll, ss, 0, 0))
    mspec = lambda ll: pl.BlockSpec((None, None, 3 * N_SUB, D), lambda b, i: (ll, b, 0, 0))
    if nxt is None:
        return pl.pallas_call(
            functools.partial(_post_last_kernel, sub=sub, coef=coef),
            out_shape=jax.ShapeDtypeStruct((B, S, D), F32),
            grid=(B, S // ts),
            in_specs=[xspec, xspec, gspec(l, sub), mspec(l)],
            out_specs=xspec,
            compiler_params=_cparams(("parallel", "parallel")),
            name="post_last",
        )(x, y, norm_post, mod), None
    l2, sub2 = nxt
    return pl.pallas_call(
        functools.partial(_post_kernel, sub=sub, coef=coef, nsub=sub2),
        out_shape=(jax.ShapeDtypeStruct((B, S, D), F32), jax.ShapeDtypeStruct((B, S, D), BF16)),
        grid=(B, S // ts),
        in_specs=[xspec, xspec, gspec(l, sub), mspec(l), gspec(l2, sub2), mspec(l2)],
        out_specs=(xspec, xspec),
        compiler_params=_cparams(("parallel", "parallel")),
        name="post_residual",
    )(x, y, norm_post, mod, norm_pre, mod)


def _mm_kernel(a_ref, w_ref, o_ref, *, act):
    acc = jnp.dot(a_ref[...], w_ref[...], preferred_element_type=F32)
    if act == "sigmoid":
        acc = jax.nn.sigmoid(acc)
    o_ref[...] = acc.astype(o_ref.dtype)


def matmul(a, w, l, tm, tn, out_dtype, act=None, name="matmul", col0=0, n_cols=None):
    M, K = a.shape
    N = w.shape[-1] if n_cols is None else n_cols
    assert col0 % tn == 0 and N % tn == 0
    j0 = col0 // tn
    return pl.pallas_call(
        functools.partial(_mm_kernel, act=act),
        out_shape=jax.ShapeDtypeStruct((M, N), out_dtype),
        grid=(M // tm, N // tn),
        in_specs=[
            pl.BlockSpec((tm, K), lambda i, j: (i, 0)),
            pl.BlockSpec((None, K, tn), lambda i, j: (l, 0, j0 + j)),
        ],
        out_specs=pl.BlockSpec((tm, tn), lambda i, j: (i, j)),
        compiler_params=_cparams(("parallel", "arbitrary")),
        name=name,
    )(a, w)


def _swiglu_kernel(a_ref, wa_ref, wb_ref, o_ref):
    h = a_ref[...]
    a = jnp.dot(h, wa_ref[...], preferred_element_type=F32)
    b = jnp.dot(h, wb_ref[...], preferred_element_type=F32)
    o_ref[...] = (_silu(a) * b).astype(o_ref.dtype)


def swiglu_in(h, w, l, tm, tn):
    M, K = h.shape
    F = w.shape[-1] // 2
    nb = F // tn
    return pl.pallas_call(
        _swiglu_kernel,
        out_shape=jax.ShapeDtypeStruct((M, F), BF16),
        grid=(M // tm, nb),
        in_specs=[
            pl.BlockSpec((tm, K), lambda i, j: (i, 0)),
            pl.BlockSpec((None, K, tn), lambda i, j: (l, 0, j)),
            pl.BlockSpec((None, K, tn), lambda i, j: (l, 0, j + nb)),
        ],
        out_specs=pl.BlockSpec((tm, tn), lambda i, j: (i, j)),
        compiler_params=_cparams(("parallel", "arbitrary")),
        name="swiglu_in",
    )(h, w, w)


def _merge_kernel(o1_ref, o2_ref, o3_ref, w1_ref, w2_ref, w3_ref, g1_ref, g2_ref, g3_ref, o_ref):
    acc = g1_ref[...].astype(F32) * jnp.dot(o1_ref[...], w1_ref[...], preferred_element_type=F32)
    acc += g2_ref[...].astype(F32) * jnp.dot(o2_ref[...], w2_ref[...], preferred_element_type=F32)
    acc += g3_ref[...].astype(F32) * jnp.dot(o3_ref[...], w3_ref[...], preferred_element_type=F32)
    o_ref[...] = acc.astype(o_ref.dtype)


def branch_merge(o_ret, o_nsa, o_rwkv, w_ret, w_nsa, w_rwkv, zg, l, tm, tn):
    M = o_ret.shape[0]
    D = w_ret.shape[-1]
    nb = D // tn
    ospec = lambda o: pl.BlockSpec((tm, o.shape[1]), lambda i, j: (i, 0))
    wspec = lambda w: pl.BlockSpec((None, w.shape[1], tn), lambda i, j: (l, 0, j))
    gspec = lambda k: pl.BlockSpec((tm, tn), lambda i, j: (i, j + k * nb))
    return pl.pallas_call(
        _merge_kernel,
        out_shape=jax.ShapeDtypeStruct((M, D), BF16),
        grid=(M // tm, nb),
        in_specs=[ospec(o_ret), ospec(o_nsa), ospec(o_rwkv), wspec(w_ret), wspec(w_nsa), wspec(w_rwkv),
                  gspec(0), gspec(1), gspec(2)],
        out_specs=pl.BlockSpec((tm, tn), lambda i, j: (i, j)),
        compiler_params=_cparams(("parallel", "arbitrary")),
        name="branch_merge",
    )(o_ret, o_nsa, o_rwkv, w_ret, w_nsa, w_rwkv, zg, zg, zg)


def _ret_kernel(q_ref, k_ref, v_ref, g_ref, cos_ref, sin_ref, dm_ref, zeta_ref, xi_ref, o_ref, r_scr, *, decays):
    @pl.when(pl.program_id(1) == 0)
    def _():
        r_scr[...] = jnp.zeros_like(r_scr)

    cos = cos_ref[...]
    sin = sin_ref[...]
    dk, dv = RET_QK_HD, RET_V_HD
    for h in range(RET_HEADS):
        qh = q_ref[0, :, h * dk:(h + 1) * dk]
        kh = k_ref[0, :, h * dk:(h + 1) * dk]
        qh = qh * cos + pltpu.roll(qh, dk // 2, 1) * sin
        kh = (kh * cos + pltpu.roll(kh, dk // 2, 1) * sin) * (dk ** -0.5)
        vh = v_ref[0, :, h * dv:(h + 1) * dv]
        s = _dot_nt(qh, kh) * dm_ref[h]
        state = r_scr[h]
        o = _dot(s, vh) + _dot(qh, state) * xi_ref[h]
        mu = jnp.mean(o, axis=-1, keepdims=True)
        d = o - mu
        var = jnp.mean(d * d, axis=-1, keepdims=True)
        on = d * lax.rsqrt(var + 1e-6)
        gh = g_ref[0, :, h * dv:(h + 1) * dv]
        o_ref[0, :, h * dv:(h + 1) * dv] = (_silu(gh) * on).astype(o_ref.dtype)
        r_scr[h] = _dot_tn(kh * zeta_ref[h], vh) + decays[h] * state


def retention(z3):
    B, S, _ = z3.shape
    H, C = RET_HEADS, RET_CHUNK
    pos = jnp.arange(S, dtype=F32)
    inv = 1.0 / (ROPE_BASE ** jnp.linspace(0.0, 1.0, RET_QK_HD // 2))
    ang = pos[:, None] * inv[None, :]
    cos, sin = jnp.cos(ang), jnp.sin(ang)
    cos_f = jnp.concatenate([cos, cos], axis=-1)
    sin_f = jnp.concatenate([-sin, sin], axis=-1)
    log_g = jnp.log1p(-(2.0 ** (-5.0 - jnp.arange(H, dtype=F32))))
    idx = jnp.arange(C, dtype=F32)
    diff = idx[:, None] - idx[None, :]
    dmask = jnp.where(diff >= 0, jnp.exp(jnp.maximum(diff, 0.0)[None] * log_g[:, None, None]), 0.0)
    zeta = jnp.exp((C - 1 - idx)[None, :] * log_g[:, None])
    xi = jnp.exp((idx + 1)[None, :] * log_g[:, None])
    zeta_t = jnp.broadcast_to(zeta[:, :, None], (H, C, RET_QK_HD))
    xi_t = jnp.broadcast_to(xi[:, :, None], (H, C, RET_V_HD))
    lg64 = np.log1p(-(2.0 ** (-5.0 - np.arange(H, dtype=np.float64))))
    decays = tuple(float(v) for v in np.exp(C * lg64))
    qb = ZM_RET // RET_QK
    vb = (ZM_RET + 2 * RET_QK) // RET_V
    const = lambda shape: pl.BlockSpec(shape, lambda b, c: (0,) * len(shape))
    return pl.pallas_call(
        functools.partial(_ret_kernel, decays=decays),
        out_shape=jax.ShapeDtypeStruct((B, S, RET_V), BF16),
        grid=(B, S // C),
        in_specs=[
            pl.BlockSpec((1, C, RET_QK), lambda b, c: (b, c, qb)),
            pl.BlockSpec((1, C, RET_QK), lambda b, c: (b, c, qb + 1)),
            pl.BlockSpec((1, C, RET_V), lambda b, c: (b, c, vb)),
            pl.BlockSpec((1, C, RET_V), lambda b, c: (b, c, vb + 1)),
            pl.BlockSpec((C, RET_QK_HD), lambda b, c: (c, 0)),
            pl.BlockSpec((C, RET_QK_HD), lambda b, c: (c, 0)),
            const((H, C, C)), const((H, C, RET_QK_HD)), const((H, C, RET_V_HD)),
        ],
        out_specs=pl.BlockSpec((1, C, RET_V), lambda b, c: (b, c, 0)),
        scratch_shapes=[pltpu.VMEM((H, RET_QK_HD, RET_V_HD), F32)],
        compiler_params=_cparams(("parallel", "arbitrary")),
        name="retention",
    )(z3, z3, z3, z3, cos_f, sin_f, dmask, zeta_t, xi_t)


def _rel_bucket(dist):
    n = jnp.maximum(dist, 0)
    max_exact = REL_BUCKETS // 2
    nf = jnp.maximum(n, 1).astype(F32)
    large = max_exact + (jnp.log(nf / max_exact) / math.log(REL_MAX_DIST / max_exact)
                         * (REL_BUCKETS - max_exact)).astype(jnp.int32)
    large = jnp.minimum(large, REL_BUCKETS - 1)
    return jnp.where(n < max_exact, n, large)


def _cmp_kernel(kc_ref, vc_ref, pos_ref, w1_ref, b1_ref, w2_ref, b2_ref, ko_ref, vo_ref, *, nb):
    d = NSA_HD
    half = CMP_BLOCK // 2
    for i, (src, dst) in enumerate(((kc_ref, ko_ref), (vc_ref, vo_ref))):
        p1 = jnp.zeros((nb, w1_ref.shape[-1]), F32)
        p2 = jnp.zeros((nb, w1_ref.shape[-1]), F32)
        for t in range(half):
            a = src[0, pl.ds(t, nb, stride=CMP_STRIDE), :]
            p1 += _dot(a + pos_ref[i, t:t + 1, :], w1_ref[i, t * d:(t + 1) * d, :])
            p2 += _dot(a + pos_ref[i, half + t:half + t + 1, :], w1_ref[i, (half + t) * d:(half + t + 1) * d, :])
        pre = p1 + pltpu.roll(p2, nb - 1, 0) + b1_ref[i]
        dst[0, 0] = _dot(jax.nn.gelu(pre), w2_ref[i]) + b2_ref[i]


def nsa_compress(z3, cmp_pos, cmp_w1, cmp_b1, cmp_w2, cmp_b2):
    B, S, _ = z3.shape
    G, d = NSA_KV_HEADS, NSA_HD
    nb = S // CMP_STRIDE
    kb = ZM_NSA_KV // d
    hid = cmp_w1.shape[-1]
    const = lambda shape: pl.BlockSpec(shape, lambda b, g: (0,) * len(shape))
    out = jax.ShapeDtypeStruct((B, G, nb, d), F32)
    return pl.pallas_call(
        functools.partial(_cmp_kernel, nb=nb),
        out_shape=(out, out),
        grid=(B, G),
        in_specs=[
            pl.BlockSpec((1, S, d), lambda b, g: (b, 0, kb + g)),
            pl.BlockSpec((1, S, d), lambda b, g: (b, 0, kb + G + g)),
            const((2, CMP_BLOCK, d)), const((2, CMP_BLOCK * d, hid)), const((2, 1, hid)),
            const((2, hid, d)), const((2, 1, d)),
        ],
        out_specs=(pl.BlockSpec((1, 1, nb, d), lambda b, g: (b, g, 0, 0)),
                   pl.BlockSpec((1, 1, nb, d), lambda b, g: (b, g, 0, 0))),
        compiler_params=_cparams(("parallel", "parallel")),
        name="nsa_compress",
    )(z3, z3, cmp_pos, cmp_w1, cmp_b1.reshape(2, 1, hid), cmp_w2, cmp_b2.reshape(2, 1, d))


def _nsa_kernel(q_ref, gate_ref, kc_ref, vc_ref, ks_ref, vs_ref, kw_ref, vw_ref, bc_ref, bs_ref, bw_ref, ovt_ref, e_ref,
                o_ref, q_scr, os_scr, ksf, vsf, kwf, vwf, *, n_sblk, nkb):
    g = pl.program_id(1)
    qi = pl.program_id(2)
    T = LANES
    HG = NSA_GROUP
    nwb = WINDOW // T + 1
    spb = T // SEL_BLOCK

    @pl.when(qi == 0)
    def _():
        for src, dst in ((ks_ref, ksf), (vs_ref, vsf), (kw_ref, kwf), (vw_ref, vwf)):
            for blk in range(nkb):
                dst[(nkb - 1 - blk) * T:(nkb - blk) * T, :] = src[0, blk * T:(blk + 1) * T, :].astype(BF16)
            dst[nkb * T:, :] = jnp.zeros(((nkb - 1) * T, NSA_HD), BF16)

    for j in range(HG):
        q_scr[j * T:(j + 1) * T, :] = (q_ref[0, :, j * T:(j + 1) * T] * (NSA_HD ** -0.5)).astype(BF16)
    q_all = q_scr[...]
    start = pl.multiple_of((nkb - 1 - qi) * T, T)

    def softmax_pv(s, v):
        m = jnp.max(s, axis=-1, keepdims=True)
        p = jnp.exp(s - m)
        den = jnp.sum(p, axis=-1, keepdims=True)
        o = jnp.dot(p.reshape(HG * T, p.shape[-1]).astype(BF16), v, preferred_element_type=F32)
        return o.reshape(HG, T, v.shape[-1]) / den

    s = _dot_nt(q_all, kc_ref[0, 0]).reshape(HG, T, T) + bc_ref[...]
    m = jnp.max(s, axis=-1, keepdims=True)
    p = jnp.where(s > 0.5 * NEG, jnp.exp(s - m), 0.0)
    den = jnp.sum(p, axis=-1, keepdims=True)
    pn = p / jnp.where(den > 0.0, den, 1.0)
    o_c = _dot(pn.reshape(HG * T, T), vc_ref[0, 0]).reshape(HG, T, NSA_HD)
    psum = pn[0]
    for j in range(1, HG):
        psum = psum + pn[j]

    p_hi = psum.astype(BF16)
    p_lo = (psum - p_hi.astype(F32)).astype(BF16)
    imp = (_dot_nt(ovt_ref[...], p_hi) + _dot_nt(ovt_ref[...], p_lo))[0:n_sblk, :]
    blk = lax.broadcasted_iota(jnp.int32, (n_sblk, T), 0)
    lane = lax.broadcasted_iota(jnp.int32, (n_sblk, T), 1)
    cur = (qi * T + lane) // SEL_BLOCK
    forced = (blk == 0) | (blk == cur) | (blk == cur - 1)
    impm = jnp.where(forced, 1e30, jnp.where(blk <= cur, imp, NEG))
    rank = jnp.zeros((n_sblk, T), F32)
    for mb in range(n_sblk):
        cm = impm[mb:mb + 1, :]
        lower = jnp.where(blk > mb, 1.0, 0.0)
        rank += jnp.where(cm > impm, 1.0, jnp.where(cm == impm, lower, 0.0))
    sel_t = jnp.where(rank < float(min(N_SEL, n_sblk)), 1.0, 0.0)

    rev = jnp.where(blk + lane == spb * qi + spb - 1, 1.0, 0.0)
    sel_back = _dot_tn(sel_t, rev).astype(BF16)

    def selected(ncol):
        keep = jnp.dot(sel_back, e_ref[:, 0:ncol], preferred_element_type=F32)
        s = _dot_nt(q_all, ksf[pl.ds(start, ncol), :]).reshape(HG, T, ncol) + bs_ref[:, :, 0:ncol]
        os_scr[...] = softmax_pv(s + ((keep - 1.0) * -NEG)[None], vsf[pl.ds(start, ncol), :])

    quarter = max(nkb // 4, 1)
    bounds = list(range(quarter, nkb, quarter)) + [nkb]
    for lo_blk, hi_blk in zip([0] + bounds[:-1], bounds):
        pl.when((qi >= lo_blk) & (qi < hi_blk))(functools.partial(selected, hi_blk * T))
    o_s = os_scr[...]

    colw = lax.broadcasted_iota(jnp.int32, (T, nwb * T), 1)
    s = _dot_nt(q_all, kwf[pl.ds(start, nwb * T), :]).reshape(HG, T, nwb * T) + bw_ref[...]
    o_w = softmax_pv(s + jnp.where(colw < (qi + 1) * T, 0.0, NEG)[None], vwf[pl.ds(start, nwb * T), :])

    gates = jax.nn.sigmoid(gate_ref[0])
    for j in range(HG):
        base = 3 * j
        gsel = lambda c: jnp.where(g == 0, gates[:, c:c + 1], gates[:, 3 * HG + c:3 * HG + c + 1])
        o = gsel(base) * o_c[j] + gsel(base + 1) * o_s[j] + gsel(base + 2) * o_w[j]
        o_ref[0, :, j * T:(j + 1) * T] = o.astype(o_ref.dtype)


def _shifted_rows(v, n_rows, step):
    H, P = v.shape
    reps = -(-(n_rows * (P + step)) // P)
    return jnp.tile(v, (1, reps))[:, :n_rows * (P + step)].reshape(H, n_rows, P + step)


def nsa_tables(rel_bias, S):
    T = LANES
    n_cmp = (S - CMP_BLOCK) // CMP_STRIDE + 1
    n_sblk = S // SEL_BLOCK
    nkb = S // T
    nwb = WINDOW // T + 1
    assert n_cmp < T and n_sblk <= T and S // CMP_STRIDE == T and nwb <= nkb
    H = rel_bias.shape[1]
    tab = rel_bias[_rel_bucket(jnp.arange(S + T)), :].T
    ii = np.arange(T)[:, None]
    cc = np.arange(nkb * T)[None, :]
    dist = ii + T * (cc // T) - cc % T
    tab_p = jnp.pad(tab, ((0, 0), (T - 1, 0)))
    hank = _shifted_rows(tab_p, T, 1)[:, :, :nkb * T]
    bias = hank.reshape(H, T, nkb, T)[..., ::-1].reshape(H, T, nkb * T)
    bias_s = jnp.where(dist >= 0, bias, NEG)
    bias_w = jnp.where((dist >= 0) & (dist < WINDOW), bias, NEG)[:, :, :nwb * T]
    ends = CMP_STRIDE * np.arange(T) + CMP_BLOCK - 1
    valid_c = ((np.arange(S)[:, None] - ends[None, :]) >= 0) & (np.arange(T)[None, :] < n_cmp)
    tab_q = jnp.pad(tab, ((0, 0), (int(ends[-1]), 0)))
    rows = _shifted_rows(tab_q, T, CMP_STRIDE)[:, ::-1, :S]
    bias_c = jnp.where(valid_c, rows.transpose(0, 2, 1), NEG)

    cmp_start = CMP_STRIDE * np.arange(T)
    sel_start = SEL_BLOCK * np.arange(T)
    ov = ((cmp_start[:, None] <= (sel_start + SEL_BLOCK - 1)[None, :])
          & ((cmp_start + CMP_BLOCK - 1)[:, None] >= sel_start[None, :])
          & (np.arange(T)[:, None] < n_cmp) & (np.arange(T)[None, :] < n_sblk))
    spb = T // SEL_BLOCK
    c = np.arange(nkb * T)
    back = spb * (c // T) + spb - 1 - (c % T) // SEL_BLOCK
    expand = np.arange(T)[:, None] == back[None, :]
    return bias_c, bias_s, bias_w, jnp.asarray(ov.T, BF16), jnp.asarray(expand, BF16)


def nsa_attention(zm, zt, kcmp, vcmp, tables):
    B, S, _ = zm.shape
    G, HG, d, T = NSA_KV_HEADS, NSA_GROUP, NSA_HD, LANES
    assert G == 2 and d == T
    n_sblk = S // SEL_BLOCK
    nkb = S // T
    nwb = WINDOW // T + 1
    bias_c, bias_s, bias_w, ov_t, expand = tables

    qb = ZM_NSA_Q // (HG * d)
    kvb = ZM_NSA_KV // d
    kvspec = lambda off: pl.BlockSpec((1, S, d), lambda b, g, i: (b, 0, kvb + off * G + g))
    cspec = pl.BlockSpec((1, 1, T, d), lambda b, g, i: (b, g, 0, 0))
    flip = pltpu.VMEM(((2 * nkb - 1) * T, d), BF16)
    return pl.pallas_call(
        functools.partial(_nsa_kernel, n_sblk=n_sblk, nkb=nkb),
        out_shape=jax.ShapeDtypeStruct((B, S, NSA_Q), BF16),
        grid=(B, G, nkb),
        in_specs=[
            pl.BlockSpec((1, T, HG * d), lambda b, g, i: (b, i, qb + g)),
            pl.BlockSpec((1, T, NSA_GATE_PAD), lambda b, g, i: (b, i, ZT_GATE // NSA_GATE_PAD)),
            cspec, cspec, kvspec(2), kvspec(3), kvspec(4), kvspec(5),
            pl.BlockSpec((HG, T, T), lambda b, g, i: (g, i, 0)),
            pl.BlockSpec((HG, T, nkb * T), lambda b, g, i: (g, 0, 0)),
            pl.BlockSpec((HG, T, nwb * T), lambda b, g, i: (g, 0, 0)),
            pl.BlockSpec((T, T), lambda b, g, i: (0, 0)),
            pl.BlockSpec((T, nkb * T), lambda b, g, i: (0, 0)),
        ],
        out_specs=pl.BlockSpec((1, T, HG * d), lambda b, g, i: (b, i, g)),
        scratch_shapes=[pltpu.VMEM((HG * T, d), BF16), pltpu.VMEM((HG, T, d), F32), flip, flip, flip, flip],
        compiler_params=_cparams(("parallel", "parallel", "arbitrary")),
        name="nsa_attention",
    )(zm, zt, kcmp, vcmp, zm, zm, zm, zm, bias_c, bias_s, bias_w, ov_t, expand)


def _rwkv_prep_kernel(zr_ref, zk_ref, zv_ref, zl_ref, mu_ref, mul_ref, vec_ref, w2_ref, a2_ref, g2_ref,
                      r_ref, k_ref, v_ref, lw_ref, kk_ref, kka_ref, gg_ref, c_scr, cl_scr):
    first = pl.program_id(1) == 0
    tb = zr_ref.shape[1]
    row = lax.broadcasted_iota(jnp.int32, (tb, 1), 0)

    def shift(z, mu, carry_ref, slot):
        prev_last = jnp.where(first, 0.0, carry_ref[slot:slot + 1, :])
        prev = jnp.where(row == 0, prev_last, pltpu.roll(z, 1, 0))
        carry_ref[slot:slot + 1, :] = z[tb - 1:tb, :]
        return z + (prev - z) * mu

    r = shift(zr_ref[0], mu_ref[0:1, :], c_scr, 0)
    k = shift(zk_ref[0], mu_ref[1:2, :], c_scr, 1)
    v = shift(zv_ref[0], mu_ref[2:3, :], c_scr, 2)
    xl = shift(zl_ref[0], mul_ref[...], cl_scr, 0)
    w0, a0, k_k, k_a = vec_ref[0:1, :], vec_ref[1:2, :], vec_ref[2:3, :], vec_ref[3:4, :]
    u = -(w0 + _dot(jnp.tanh(xl), w2_ref[...]))
    w_log = -(jnp.maximum(u, 0.0) + jnp.log1p(jnp.exp(-jnp.abs(u)))) - 0.5
    a = jax.nn.sigmoid(a0 + _dot(xl, a2_ref[...]))
    r_ref[0] = r
    k_ref[0] = k * (1.0 + (a - 1.0) * k_a)
    v_ref[0] = v
    lw_ref[0] = -jnp.exp(w_log)
    kk = k * k_k
    kk_ref[0] = kk
    kka_ref[0] = kk * a
    gg_ref[0] = _dot(jax.nn.sigmoid(xl), g2_ref[...])


def rwkv_prep(z3, mu_rkv, mu_lora, vecs, w2p, a2p, g2p, tb):
    B, S, _ = z3.shape
    Dm = RWKV_DIM
    rb = ZT_RWKV // Dm
    zspec = lambda o: pl.BlockSpec((1, tb, Dm), lambda b, i: (b, i, rb + o))
    const = lambda shape: pl.BlockSpec(shape, lambda b, i: (0,) * len(shape))
    ospec = pl.BlockSpec((1, tb, Dm), lambda b, i: (b, i, 0))
    out = jax.ShapeDtypeStruct((B, S, Dm), F32)
    return pl.pallas_call(
        _rwkv_prep_kernel,
        out_shape=(out,) * 7,
        grid=(B, S // tb),
        in_specs=[zspec(0), zspec(1), zspec(2),
                  pl.BlockSpec((1, tb, LORA_PAD), lambda b, i: (b, i, ZT_LORA // LORA_PAD)),
                  const((3, Dm)), const((1, LORA_PAD)), const((4, Dm)),
                  const((LORA_PAD, Dm)), const((LORA_PAD, Dm)), const((LORA_PAD, Dm))],
        out_specs=(ospec,) * 7,
        scratch_shapes=[pltpu.VMEM((8, Dm), F32), pltpu.VMEM((8, LORA_PAD), F32)],
        compiler_params=_cparams(("parallel", "arbitrary")),
        name="rwkv_prep",
    )(z3, z3, z3, z3, mu_rkv, mu_lora, vecs, w2p, a2p, g2p)


def _rwkv_masks():
    T, n = RWKV_CHUNK, RWKV_GW
    idx = np.arange(n)
    h, t = idx // T, idx % T
    same = h[:, None] == h[None, :]
    tt, ss = t[:, None], t[None, :]
    levels = []
    b = 1
    while b < T:
        levels.append(same & (tt // (2 * b) == ss // (2 * b)) & (tt % (2 * b) >= b) & (ss % (2 * b) < b))
        b *= 2
    masks = [same, same & (ss < tt), same & (ss <= tt), np.eye(n, dtype=bool)] + levels
    return np.stack(masks).astype(np.float32), len(levels)


def _rwkv_chunk_kernel(r_ref, k_ref, v_ref, lw_ref, kk_ref, kka_ref, gg_ref, rk_ref, ln_ref, msk_ref, tri_ref,
                       o_ref, s_scr, *, n_levels):
    T, GW = RWKV_CHUNK, RWKV_GW

    @pl.when(pl.program_id(1) == 0)
    def _():
        s_scr[...] = jnp.zeros_like(s_scr)

    m_bd = msk_ref[0]
    m_strict = msk_ref[1]
    m_incl = msk_ref[2]
    eye = msk_ref[3].astype(F32)
    tri = tri_ref[...]

    def to_bd(x):
        return jnp.concatenate([x.astype(BF16)] * RWKV_GROUP, axis=0) * m_bd

    def from_bd(y):
        out = y[0:T]
        for i in range(1, RWKV_GROUP):
            out = out + y[i * T:(i + 1) * T]
        return out

    def seg_sums(xs):
        x = jnp.concatenate(xs, axis=0)
        hi = x.astype(BF16)
        lo = (x - hi.astype(F32)).astype(BF16)
        n = x.shape[0]
        both = mm(jnp.concatenate([hi, lo], axis=0), m_bd)
        out = both[0:n] + both[n:2 * n]
        return [out[i * T:(i + 1) * T] for i in range(len(xs))]

    def mm(a, b):
        return jnp.dot(a, b, preferred_element_type=F32)

    def mm_nt(a, b):
        return lax.dot_general(a, b, (((1,), (1,)), ((), ())), preferred_element_type=F32)

    def mm_tn(a, b):
        return lax.dot_general(a, b, (((0,), (0,)), ((), ())), preferred_element_type=F32)

    groups = range(RWKV_DIM // GW)
    sls = [slice(gi * GW, (gi + 1) * GW) for gi in groups]
    rs = [r_ref[0, :, sl] for sl in sls]
    ks = [k_ref[0, :, sl] for sl in sls]
    vs = [v_ref[0, :, sl] for sl in sls]
    lws = [lw_ref[0, :, sl] for sl in sls]
    sums = seg_sums([kk_ref[0, :, sl] * kk_ref[0, :, sl] for sl in sls]
                    + [r * k * rk_ref[:, sl] for r, k, sl in zip(rs, ks, sls)])
    inv_n = [1.0 / jnp.maximum(jnp.sqrt(ss), 1e-12) for ss in sums[:len(sls)]]
    bonus = [rk_sum * v for rk_sum, v in zip(sums[len(sls):], vs)]
    cum = [_split_dot_left(tri, lw) for lw in lws]
    p_in = [jnp.exp(c) for c in cum]
    p_inv = [jnp.exp(-c) for c in cum]
    a_bd = [to_bd(-(kk_ref[0, :, sl] * n) * jnp.exp(c - lw)) for sl, n, c, lw in zip(sls, inv_n, cum, lws)]
    r_bd = [to_bd(r * p) for r, p in zip(rs, p_in)]
    b_bd = [to_bd(kka_ref[0, :, sl] * n * p) for sl, n, p in zip(sls, inv_n, p_inv)]
    k_bd = [to_bd(k * p) for k, p in zip(ks, p_inv)]
    v_bd = [to_bd(v) for v in vs]

    ar_bd = [jnp.concatenate([a, r], axis=0) for a, r in zip(a_bd, r_bd)]
    on_b = [mm_nt(ar, b) for ar, b in zip(ar_bd, b_bd)]
    on_k = [mm_nt(ar, k) for ar, k in zip(ar_bd, k_bd)]
    a_ab = [(t[0:GW] * m_strict).astype(BF16) for t in on_b]
    a_rb = [(t[GW:2 * GW] * m_incl).astype(BF16) for t in on_b]
    a_k = [jnp.concatenate([(t[0:GW] * m_strict).astype(BF16), (t[GW:2 * GW] * m_incl).astype(BF16)], axis=0)
           for t in on_k]

    x = [eye + a * msk_ref[4] for a in a_ab]
    for lv in range(1, n_levels):
        xb = [xi.astype(BF16) for xi in x]
        t = [mm(xi, a * msk_ref[4 + lv]).astype(BF16) for xi, a in zip(xb, a_ab)]
        x = [xf + mm(ti, xi) for xf, ti, xi in zip(x, t, xb)]
    xb = [xi.astype(BF16) for xi in x]

    state = [s_scr[gi] for gi in groups]
    sb = [s.astype(BF16) for s in state]
    on_s = [mm_nt(ar, s) + mm(ak, v) for ar, s, ak, v in zip(ar_bd, sb, a_k, v_bd)]
    u = [mm(xi, t[0:GW].astype(BF16)).astype(BF16) for xi, t in zip(xb, on_s)]
    y = [from_bd(t[GW:2 * GW] + mm(arb, ui)) for t, arb, ui in zip(on_s, a_rb, u)]
    for gi in groups:
        s_scr[gi] = (state[gi] + mm_tn(u[gi], b_bd[gi]) + mm_tn(v_bd[gi], k_bd[gi])) * p_in[gi][T - 1:T, :]

    mu = [m * (1.0 / RWKV_HD) for m in seg_sums(y)]
    d = [yi - m for yi, m in zip(y, mu)]
    var = [s2 * (1.0 / RWKV_HD) for s2 in seg_sums([di * di for di in d])]
    for gi, sl in enumerate(sls):
        yn = d[gi] * lax.rsqrt(var[gi] + RWKV_LN_EPS) * ln_ref[0:1, sl] + ln_ref[1:2, sl]
        o_ref[0, :, sl] = ((yn + bonus[gi]) * gg_ref[0, :, sl]).astype(o_ref.dtype)


def _split_dot_left(w_bf16, x):
    acc = None
    rem = x
    for _ in range(3):
        hi = rem.astype(BF16)
        t = jnp.dot(w_bf16, hi, preferred_element_type=F32)
        acc = t if acc is None else acc + t
        rem = rem - hi.astype(F32)
    return acc


def rwkv_chunk(r, k, v, lw, kk, kka, gg, rk, ln):
    B, S, Dm = r.shape
    T = RWKV_CHUNK
    masks, n_levels = _rwkv_masks()
    tri = np.tril(np.ones((T, T), np.float32))
    ng = Dm // RWKV_GW
    xspec = pl.BlockSpec((1, T, Dm), lambda b, c: (b, c, 0))
    const = lambda shape: pl.BlockSpec(shape, lambda b, c: (0,) * len(shape))
    return pl.pallas_call(
        functools.partial(_rwkv_chunk_kernel, n_levels=n_levels),
        out_shape=jax.ShapeDtypeStruct((B, S, Dm), BF16),
        grid=(B, S // T),
        in_specs=[xspec] * 7 + [const((1, Dm)), const((2, Dm)), const(masks.shape), const((T, T))],
        out_specs=xspec,
        scratch_shapes=[pltpu.VMEM((ng, RWKV_GW, RWKV_GW), F32)],
        compiler_params=_cparams(("parallel", "arbitrary")),
        name="rwkv_chunk",
    )(r, k, v, lw, kk, kka, gg, rk, ln, jnp.asarray(masks, BF16), jnp.asarray(tri, BF16))


def _cast_kernel(x_ref, o_ref):
    o_ref[...] = x_ref[...].astype(o_ref.dtype)


def cast_leading_cols(w, n_cols, tr, tc):
    L, K, _ = w.shape
    spec = pl.BlockSpec((None, tr, tc), lambda l, i, j: (l, i, j))
    return pl.pallas_call(
        _cast_kernel,
        out_shape=jax.ShapeDtypeStruct((L, K, n_cols), BF16),
        grid=(L, K // tr, n_cols // tc),
        in_specs=[spec],
        out_specs=spec,
        compiler_params=_cparams(("parallel", "parallel", "parallel")),
        name="cast_w_main",
    )(w)


def _prep_weights(w_in, rwkv_mu, rwkv_w2, rwkv_a2, rwkv_g2):
    L, D, _ = w_in.shape
    o = 3 * D + ZM_COLS
    w_main = cast_leading_cols(w_in, o, min(2048, D), 512)
    nsa_gate = w_in[:, :, o:o + 3 * NSA_HEADS]
    o += 3 * NSA_HEADS
    rkv = w_in[:, :, o:o + 3 * RWKV_DIM]
    lora = w_in[:, :, o + 3 * RWKV_DIM:o + RWKV_COLS]
    pad = lambda w, n: jnp.pad(w, ((0, 0), (0, 0), (0, n - w.shape[-1])))
    w_tail = jnp.concatenate([rkv, pad(lora, LORA_PAD), pad(nsa_gate, NSA_GATE_PAD)], axis=-1).astype(BF16)
    mu_rkv = rwkv_mu[:, :3 * RWKV_DIM].reshape(L, 3, RWKV_DIM)
    mu_lora = jnp.pad(rwkv_mu[:, 3 * RWKV_DIM:], ((0, 0), (0, LORA_PAD - LORA_COLS))).reshape(L, 1, LORA_PAD)
    rows = lambda w, start: jnp.pad(w, ((0, 0), (start, LORA_PAD - start - w.shape[1]), (0, 0)))
    w2p = rows(rwkv_w2, 0)
    a2p = rows(rwkv_a2, DECAY_LORA)
    g2p = rows(rwkv_g2, DECAY_LORA + AAA_LORA)
    return w_main, w_tail, mu_rkv, mu_lora, w2p, a2p, g2p


def kernel(x, c, rel_bias, w_in, w_branch_ret, w_branch_nsa, w_branch_rwkv, w_out, ffn1_in, ffn1_out, ffn2_in,
           ffn2_out, ada_down, ada_up, ada_bias, norm_pre, norm_post, cmp_pos, cmp_w1, cmp_b1, cmp_w2, cmp_b2,
           rwkv_mu, rwkv_vecs, rwkv_w2, rwkv_a2, rwkv_g2, rwkv_rk, rwkv_ln):
    B, S, D = x.shape
    L = w_in.shape[0]
    M = B * S
    tm = min(1024, M)
    ts = min(256, S)
    tb = min(256, S)
    tn_gate = _col_tile(3 * D, 1024)
    tn_mix = _col_tile(math.gcd(3 * D, ZM_COLS), 1024)
    tn_tail = _col_tile(ZT_COLS, 512)
    tn_d = _col_tile(D, 1024)

    w_main, w_tail, mu_rkv, mu_lora, w2p, a2p, g2p = _prep_weights(w_in, rwkv_mu, rwkv_w2, rwkv_a2, rwkv_g2)
    tables = nsa_tables(rel_bias, S)
    w_bret, w_bnsa, w_brwkv = w_branch_ret.astype(BF16), w_branch_nsa.astype(BF16), w_branch_rwkv.astype(BF16)
    w_o = w_out.astype(BF16)
    f1i, f1o, f2i, f2o = (w.astype(BF16) for w in (ffn1_in, ffn1_out, ffn2_in, ffn2_out))
    npre = norm_pre.reshape(L, N_SUB, 1, D)
    npost = norm_post.reshape(L, N_SUB, 1, D)
    rk = rwkv_rk.reshape(L, 1, RWKV_DIM)

    mod = ada_mod(c, ada_down, ada_up, ada_bias)
    h = norm_modulate(x, npre, mod, 0, 0, ts)
    for l in range(L):
        u = swiglu_in(h.reshape(M, D), f1i, l, tm, 512)
        y = matmul(u, f1o, l, tm, tn_d, F32, name="ffn_out")
        x, h = post_residual(x, y.reshape(B, S, D), npost, npre, mod, l, 0, 0.5, (l, 1), ts)
        h2 = h.reshape(M, D)
        zg = matmul(h2, w_main, l, tm, tn_gate, BF16, act="sigmoid", name="gate_proj", n_cols=3 * D)
        zm = matmul(h2, w_main, l, tm, tn_mix, F32, name="mix_proj", col0=3 * D, n_cols=ZM_COLS)
        zm = zm.reshape(B, S, ZM_COLS)
        zt = matmul(h2, w_tail, l, tm, tn_tail, F32, name="tail_proj").reshape(B, S, ZT_COLS)
        o_ret = retention(zm)
        kcmp, vcmp = nsa_compress(zm, cmp_pos[l], cmp_w1[l], cmp_b1[l], cmp_w2[l], cmp_b2[l])
        o_nsa = nsa_attention(zm, zt, kcmp, vcmp, tables)
        rw = rwkv_prep(zt, mu_rkv[l], mu_lora[l], rwkv_vecs[l], w2p[l], a2p[l], g2p[l], tb)
        o_rwkv = rwkv_chunk(*rw, rk[l], rwkv_ln[l])
        merged = branch_merge(o_ret.reshape(M, RET_V), o_nsa.reshape(M, NSA_Q), o_rwkv.reshape(M, RWKV_DIM),
                              w_bret, w_bnsa, w_brwkv, zg, l, tm, 512)
        y = matmul(merged, w_o, l, tm, tn_d, F32, name="out_proj")
        x, h = post_residual(x, y.reshape(B, S, D), npost, npre, mod, l, 1, 1.0, (l, 2), ts)
        u = swiglu_in(h.reshape(M, D), f2i, l, tm, 512)
        y = matmul(u, f2o, l, tm, tn_d, F32, name="ffn_out")
        nxt = (l + 1, 0) if l + 1 < L else None
        x, h = post_residual(x, y.reshape(B, S, D), npost, npre, mod, l, 2, 0.5, nxt, ts)
    return x
```

```python
import functools
import math

import numpy as np
import jax
import jax.numpy as jnp
from jax import lax
from jax.experimental import pallas as pl
from jax.experimental.pallas import tpu as pltpu

F32 = jnp.float32
BF16 = jnp.bfloat16

D_MODEL = 4096
DEPTH = 4
D_FF = 3072
N_SUB = 3
RET_HEADS, RET_QK_HD, RET_V_HD, RET_CHUNK = 8, 128, 256, 128
RET_QK = RET_HEADS * RET_QK_HD
RET_V = RET_HEADS * RET_V_HD
ROPE_BASE = 10000.0
NSA_HEADS, NSA_KV_HEADS, NSA_HD = 8, 2, 128
NSA_GROUP = NSA_HEADS // NSA_KV_HEADS
NSA_Q = NSA_HEADS * NSA_HD
NSA_KV = NSA_KV_HEADS * NSA_HD
CMP_BLOCK, CMP_STRIDE = 32, 16
SEL_BLOCK, N_SEL = 64, 8
WINDOW = 512
RWKV_HD, RWKV_DIM = 64, 1024
RWKV_HEADS = RWKV_DIM // RWKV_HD
DECAY_LORA, AAA_LORA, GATE_LORA = 64, 64, 160
RWKV_LN_EPS = 64e-5
REL_BUCKETS, REL_MAX_DIST = 32, 128
RET_COLS = 2 * RET_QK + 2 * RET_V
NSA_COLS = NSA_Q + 6 * NSA_KV + 3 * NSA_HEADS
RWKV_COLS = 3 * RWKV_DIM + DECAY_LORA + AAA_LORA + GATE_LORA
NEG = -1e30

LANES = 128
VMEM_LIMIT = 56 * 1024 * 1024

ZM_RET = 0
ZM_NSA_Q = RET_COLS
ZM_NSA_KV = ZM_NSA_Q + NSA_Q
ZM_COLS = ZM_NSA_KV + 6 * NSA_KV
ZT_RWKV = 0
ZT_LORA = 3 * RWKV_DIM
LORA_COLS = DECAY_LORA + AAA_LORA + GATE_LORA
LORA_PAD = 3 * LANES
ZT_GATE = ZT_LORA + LORA_PAD
NSA_GATE_PAD = LANES
ZT_COLS = ZT_GATE + NSA_GATE_PAD

RWKV_CHUNK = 64
RWKV_GROUP = 4
RWKV_GW = RWKV_GROUP * RWKV_HD


def _col_tile(n, pref):
    return max(t for t in range(LANES, min(n, pref) + 1, LANES) if n % t == 0)


def _cparams(sem):
    return pltpu.CompilerParams(dimension_semantics=sem, vmem_limit_bytes=VMEM_LIMIT)


def _dot(a, b):
    return jnp.dot(a.astype(BF16), b.astype(BF16), preferred_element_type=F32)


def _dot_nt(a, b):
    return lax.dot_general(a.astype(BF16), b.astype(BF16), (((1,), (1,)), ((), ())),
                           preferred_element_type=F32)


def _dot_tn(a, b):
    return lax.dot_general(a.astype(BF16), b.astype(BF16), (((0,), (0,)), ((), ())),
                           preferred_element_type=F32)


def _split_dot(x, w_bf16, parts):
    acc = None
    rem = x
    for _ in range(parts):
        hi = rem.astype(BF16)
        t = jnp.dot(hi, w_bf16, preferred_element_type=F32)
        acc = t if acc is None else acc + t
        rem = rem - hi.astype(F32)
    return acc


def _silu(x):
    return x * jax.nn.sigmoid(x)


def _ada_kernel(c_ref, down_ref, up_ref, bias_ref, o_ref):
    t = _dot(_silu(c_ref[...]), down_ref[...])
    o_ref[...] = _dot(t, up_ref[...]) + bias_ref[...]


def ada_mod(c, ada_down, ada_up, ada_bias):
    L, D, R = ada_down.shape
    B = c.shape[0]
    N = ada_up.shape[-1]
    tn = D
    out = pl.pallas_call(
        _ada_kernel,
        out_shape=jax.ShapeDtypeStruct((L, B, N), F32),
        grid=(L, N // tn),
        in_specs=[
            pl.BlockSpec((B, D), lambda l, j: (0, 0)),
            pl.BlockSpec((None, D, R), lambda l, j: (l, 0, 0)),
            pl.BlockSpec((None, R, tn), lambda l, j: (l, 0, j)),
            pl.BlockSpec((None, 1, tn), lambda l, j: (l, 0, j)),
        ],
        out_specs=pl.BlockSpec((None, B, tn), lambda l, j: (l, 0, j)),
        compiler_params=_cparams(("parallel", "parallel")),
        name="ada_mod",
    )(c, ada_down, ada_up, ada_bias.reshape(L, 1, N))
    return out.reshape(L, B, 3 * N_SUB, D)


def _rms(x, g):
    return x * lax.rsqrt(jnp.mean(x * x, axis=-1, keepdims=True) + 1e-6) * g


def _normmod_kernel(x_ref, g_ref, mod_ref, h_ref, *, sub):
    shift = mod_ref[3 * sub:3 * sub + 1, :]
    scale = mod_ref[3 * sub + 1:3 * sub + 2, :]
    h_ref[0] = (_rms(x_ref[0], g_ref[...]) * (1.0 + scale) + shift).astype(h_ref.dtype)


def norm_modulate(x, gain, mod, l, sub, ts):
    B, S, D = x.shape
    return pl.pallas_call(
        functools.partial(_normmod_kernel, sub=sub),
        out_shape=jax.ShapeDtypeStruct((B, S, D), BF16),
        grid=(B, S // ts),
        in_specs=[
            pl.BlockSpec((1, ts, D), lambda b, i: (b, i, 0)),
            pl.BlockSpec((None, None, 1, D), lambda b, i: (l, sub, 0, 0)),
            pl.BlockSpec((None, None, 3 * N_SUB, D), lambda b, i: (l, b, 0, 0)),
        ],
        out_specs=pl.BlockSpec((1, ts, D), lambda b, i: (b, i, 0)),
        compiler_params=_cparams(("parallel", "parallel")),
        name="norm_modulate",
    )(x, gain, mod)


def _post_kernel(x_ref, y_ref, gpost_ref, mod_ref, gpre_ref, modn_ref, xo_ref, h_ref, *, sub, coef, nsub):
    gate = mod_ref[3 * sub + 2:3 * sub + 3, :]
    xn = x_ref[0] + coef * (gate * _rms(y_ref[0], gpost_ref[...]))
    xo_ref[0] = xn
    shift = modn_ref[3 * nsub:3 * nsub + 1, :]
    scale = modn_ref[3 * nsub + 1:3 * nsub + 2, :]
    h_ref[0] = (_rms(xn, gpre_ref[...]) * (1.0 + scale) + shift).astype(h_ref.dtype)


def _post_last_kernel(x_ref, y_ref, gpost_ref, mod_ref, xo_ref, *, sub, coef):
    gate = mod_ref[3 * sub + 2:3 * sub + 3, :]
    xo_ref[0] = x_ref[0] + coef * (gate * _rms(y_ref[0], gpost_ref[...]))


def post_residual(x, y, norm_post, norm_pre, mod, l, sub, coef, nxt, ts):
    B, S, D = x.shape
    xspec = pl.BlockSpec((1, ts, D), lambda b, i: (b, i, 0))
    gspec = lambda ll, ss: pl.BlockSpec((None, None, 1, D), lambda b, i: (ll, ss, 0, 0))
    mspec = lambda ll: pl.BlockSpec((None, None, 3 * N_SUB, D), lambda b, i: (ll, b, 0, 0))
    if nxt is None:
        return pl.pallas_call(
            functools.partial(_post_last_kernel, sub=sub, coef=coef),
            out_shape=jax.ShapeDtypeStruct((B, S, D), F32),
            grid=(B, S // ts),
            in_specs=[xspec, xspec, gspec(l, sub), mspec(l)],
            out_specs=xspec,
            compiler_params=_cparams(("parallel", "parallel")),
            name="post_last",
        )(x, y, norm_post, mod), None
    l2, sub2 = nxt
    return pl.pallas_call(
        functools.partial(_post_kernel, sub=sub, coef=coef, nsub=sub2),
        out_shape=(jax.ShapeDtypeStruct((B, S, D), F32), jax.ShapeDtypeStruct((B, S, D), BF16)),
        grid=(B, S // ts),
        in_specs=[xspec, xspec, gspec(l, sub), mspec(l), gspec(l2, sub2), mspec(l2)],
        out_specs=(xspec, xspec),
        compiler_params=_cparams(("parallel", "parallel")),
        name="post_residual",
    )(x, y, norm_post, mod, norm_pre, mod)


def _mm_kernel(a_ref, w_ref, o_ref, *, act):
    acc = jnp.dot(a_ref[...], w_ref[...], preferred_element_type=F32)
    if act == "sigmoid":
        acc = jax.nn.sigmoid(acc)
    o_ref[...] = acc.astype(o_ref.dtype)


def matmul(a, w, l, tm, tn, out_dtype, act=None, name="matmul", col0=0, n_cols=None):
    M, K = a.shape
    N = w.shape[-1] if n_cols is None else n_cols
    assert col0 % tn == 0 and N % tn == 0
    j0 = col0 // tn
    return pl.pallas_call(
        functools.partial(_mm_kernel, act=act),
        out_shape=jax.ShapeDtypeStruct((M, N), out_dtype),
        grid=(M // tm, N // tn),
        in_specs=[
            pl.BlockSpec((tm, K), lambda i, j: (i, 0)),
            pl.BlockSpec((None, K, tn), lambda i, j: (l, 0, j0 + j)),
        ],
        out_specs=pl.BlockSpec((tm, tn), lambda i, j: (i, j)),
        compiler_params=_cparams(("parallel", "arbitrary")),
        name=name,
    )(a, w)


def _swiglu_kernel(a_ref, wa_ref, wb_ref, o_ref):
    h = a_ref[...]
    a = jnp.dot(h, wa_ref[...], preferred_element_type=F32)
    b = jnp.dot(h, wb_ref[...], preferred_element_type=F32)
    o_ref[...] = (_silu(a) * b).astype(o_ref.dtype)


def swiglu_in(h, w, l, tm, tn):
    M, K = h.shape
    F = w.shape[-1] // 2
    nb = F // tn
    return pl.pallas_call(
        _swiglu_kernel,
        out_shape=jax.ShapeDtypeStruct((M, F), BF16),
        grid=(M // tm, nb),
        in_specs=[
            pl.BlockSpec((tm, K), lambda i, j: (i, 0)),
            pl.BlockSpec((None, K, tn), lambda i, j: (l, 0, j)),
            pl.BlockSpec((None, K, tn), lambda i, j: (l, 0, j + nb)),
        ],
        out_specs=pl.BlockSpec((tm, tn), lambda i, j: (i, j)),
        compiler_params=_cparams(("parallel", "arbitrary")),
        name="swiglu_in",
    )(h, w, w)


def _merge_kernel(o1_ref, o2_ref, o3_ref, w1_ref, w2_ref, w3_ref, g1_ref, g2_ref, g3_ref, o_ref):
    acc = g1_ref[...].astype(F32) * jnp.dot(o1_ref[...], w1_ref[...], preferred_element_type=F32)
    acc += g2_ref[...].astype(F32) * jnp.dot(o2_ref[...], w2_ref[...], preferred_element_type=F32)
    acc += g3_ref[...].astype(F32) * jnp.dot(o3_ref[...], w3_ref[...], preferred_element_type=F32)
    o_ref[...] = acc.astype(o_ref.dtype)


def branch_merge(o_ret, o_nsa, o_rwkv, w_ret, w_nsa, w_rwkv, zg, l, tm, tn):
    M = o_ret.shape[0]
    D = w_ret.shape[-1]
    nb = D // tn
    ospec = lambda o: pl.BlockSpec((tm, o.shape[1]), lambda i, j: (i, 0))
    wspec = lambda w: pl.BlockSpec((None, w.shape[1], tn), lambda i, j: (l, 0, j))
    gspec = lambda k: pl.BlockSpec((tm, tn), lambda i, j: (i, j + k * nb))
    return pl.pallas_call(
        _merge_kernel,
        out_shape=jax.ShapeDtypeStruct((M, D), BF16),
        grid=(M // tm, nb),
        in_specs=[ospec(o_ret), ospec(o_nsa), ospec(o_rwkv), wspec(w_ret), wspec(w_nsa), wspec(w_rwkv),
                  gspec(0), gspec(1), gspec(2)],
        out_specs=pl.BlockSpec((tm, tn), lambda i, j: (i, j)),
        compiler_params=_cparams(("parallel", "arbitrary")),
        name="branch_merge",
    )(o_ret, o_nsa, o_rwkv, w_ret, w_nsa, w_rwkv, zg, zg, zg)


def _ret_kernel(q_ref, k_ref, v_ref, g_ref, cos_ref, sin_ref, dm_ref, zeta_ref, xi_ref, o_ref, r_scr, *, decays):
    @pl.when(pl.program_id(1) == 0)
    def _():
        r_scr[...] = jnp.zeros_like(r_scr)

    cos = cos_ref[...]
    sin = sin_ref[...]
    dk, dv = RET_QK_HD, RET_V_HD
    for h in range(RET_HEADS):
        qh = q_ref[0, :, h * dk:(h + 1) * dk]
        kh = k_ref[0, :, h * dk:(h + 1) * dk]
        qh = qh * cos + pltpu.roll(qh, dk // 2, 1) * sin
        kh = (kh * cos + pltpu.roll(kh, dk // 2, 1) * sin) * (dk ** -0.5)
        vh = v_ref[0, :, h * dv:(h + 1) * dv]
        s = _dot_nt(qh, kh) * dm_ref[h]
        state = r_scr[h]
        o = _dot(s, vh) + _dot(qh, state) * xi_ref[h]
        mu = jnp.mean(o, axis=-1, keepdims=True)
        d = o - mu
        var = jnp.mean(d * d, axis=-1, keepdims=True)
        on = d * lax.rsqrt(var + 1e-6)
        gh = g_ref[0, :, h * dv:(h + 1) * dv]
        o_ref[0, :, h * dv:(h + 1) * dv] = (_silu(gh) * on).astype(o_ref.dtype)
        r_scr[h] = _dot_tn(kh * zeta_ref[h], vh) + decays[h] * state


def retention(z3):
    B, S, _ = z3.shape
    H, C = RET_HEADS, RET_CHUNK
    pos = jnp.arange(S, dtype=F32)
    inv = 1.0 / (ROPE_BASE ** jnp.linspace(0.0, 1.0, RET_QK_HD // 2))
    ang = pos[:, None] * inv[None, :]
    cos, sin = jnp.cos(ang), jnp.sin(ang)
    cos_f = jnp.concatenate([cos, cos], axis=-1)
    sin_f = jnp.concatenate([-sin, sin], axis=-1)
    log_g = jnp.log1p(-(2.0 ** (-5.0 - jnp.arange(H, dtype=F32))))
    idx = jnp.arange(C, dtype=F32)
    diff = idx[:, None] - idx[None, :]
    dmask = jnp.where(diff >= 0, jnp.exp(jnp.maximum(diff, 0.0)[None] * log_g[:, None, None]), 0.0)
    zeta = jnp.exp((C - 1 - idx)[None, :] * log_g[:, None])
    xi = jnp.exp((idx + 1)[None, :] * log_g[:, None])
    zeta_t = jnp.broadcast_to(zeta[:, :, None], (H, C, RET_QK_HD))
    xi_t = jnp.broadcast_to(xi[:, :, None], (H, C, RET_V_HD))
    lg64 = np.log1p(-(2.0 ** (-5.0 - np.arange(H, dtype=np.float64))))
    decays = tuple(float(v) for v in np.exp(C * lg64))
    qb = ZM_RET // RET_QK
    vb = (ZM_RET + 2 * RET_QK) // RET_V
    const = lambda shape: pl.BlockSpec(shape, lambda b, c: (0,) * len(shape))
    return pl.pallas_call(
        functools.partial(_ret_kernel, decays=decays),
        out_shape=jax.ShapeDtypeStruct((B, S, RET_V), BF16),
        grid=(B, S // C),
        in_specs=[
            pl.BlockSpec((1, C, RET_QK), lambda b, c: (b, c, qb)),
            pl.BlockSpec((1, C, RET_QK), lambda b, c: (b, c, qb + 1)),
            pl.BlockSpec((1, C, RET_V), lambda b, c: (b, c, vb)),
            pl.BlockSpec((1, C, RET_V), lambda b, c: (b, c, vb + 1)),
            pl.BlockSpec((C, RET_QK_HD), lambda b, c: (c, 0)),
            pl.BlockSpec((C, RET_QK_HD), lambda b, c: (c, 0)),
            const((H, C, C)), const((H, C, RET_QK_HD)), const((H, C, RET_V_HD)),
        ],
        out_specs=pl.BlockSpec((1, C, RET_V), lambda b, c: (b, c, 0)),
        scratch_shapes=[pltpu.VMEM((H, RET_QK_HD, RET_V_HD), F32)],
        compiler_params=_cparams(("parallel", "arbitrary")),
        name="retention",
    )(z3, z3, z3, z3, cos_f, sin_f, dmask, zeta_t, xi_t)


def _rel_bucket(dist):
    n = jnp.maximum(dist, 0)
    max_exact = REL_BUCKETS // 2
    nf = jnp.maximum(n, 1).astype(F32)
    large = max_exact + (jnp.log(nf / max_exact) / math.log(REL_MAX_DIST / max_exact)
                         * (REL_BUCKETS - max_exact)).astype(jnp.int32)
    large = jnp.minimum(large, REL_BUCKETS - 1)
    return jnp.where(n < max_exact, n, large)


def _cmp_kernel(kc_ref, vc_ref, pos_ref, w1_ref, b1_ref, w2_ref, b2_ref, ko_ref, vo_ref, *, nb):
    d = NSA_HD
    half = CMP_BLOCK // 2
    for i, (src, dst) in enumerate(((kc_ref, ko_ref), (vc_ref, vo_ref))):
        p1 = jnp.zeros((nb, w1_ref.shape[-1]), F32)
        p2 = jnp.zeros((nb, w1_ref.shape[-1]), F32)
        for t in range(half):
            a = src[0, pl.ds(t, nb, stride=CMP_STRIDE), :]
            p1 += _dot(a + pos_ref[i, t:t + 1, :], w1_ref[i, t * d:(t + 1) * d, :])
            p2 += _dot(a + pos_ref[i, half + t:half + t + 1, :], w1_ref[i, (half + t) * d:(half + t + 1) * d, :])
        pre = p1 + pltpu.roll(p2, nb - 1, 0) + b1_ref[i]
        dst[0, 0] = _dot(jax.nn.gelu(pre), w2_ref[i]) + b2_ref[i]


def nsa_compress(z3, cmp_pos, cmp_w1, cmp_b1, cmp_w2, cmp_b2):
    B, S, _ = z3.shape
    G, d = NSA_KV_HEADS, NSA_HD
    nb = S // CMP_STRIDE
    kb = ZM_NSA_KV // d
    hid = cmp_w1.shape[-1]
    const = lambda shape: pl.BlockSpec(shape, lambda b, g: (0,) * len(shape))
    out = jax.ShapeDtypeStruct((B, G, nb, d), F32)
    return pl.pallas_call(
        functools.partial(_cmp_kernel, nb=nb),
        out_shape=(out, out),
        grid=(B, G),
        in_specs=[
            pl.BlockSpec((1, S, d), lambda b, g: (b, 0, kb + g)),
            pl.BlockSpec((1, S, d), lambda b, g: (b, 0, kb + G + g)),
            const((2, CMP_BLOCK, d)), const((2, CMP_BLOCK * d, hid)), const((2, 1, hid)),
            const((2, hid, d)), const((2, 1, d)),
        ],
        out_specs=(pl.BlockSpec((1, 1, nb, d), lambda b, g: (b, g, 0, 0)),
                   pl.BlockSpec((1, 1, nb, d), lambda b, g: (b, g, 0, 0))),
        compiler_params=_cparams(("parallel", "parallel")),
        name="nsa_compress",
    )(z3, z3, cmp_pos, cmp_w1, cmp_b1.reshape(2, 1, hid), cmp_w2, cmp_b2.reshape(2, 1, d))


def _nsa_kernel(q_ref, gate_ref, kc_ref, vc_ref, ks_ref, vs_ref, kw_ref, vw_ref, bc_ref, bs_ref, bw_ref, ovt_ref, e_ref,
                o_ref, q_scr, os_scr, ksf, vsf, kwf, vwf, *, n_sblk, nkb):
    g = pl.program_id(1)
    qi = pl.program_id(2)
    T = LANES
    HG = NSA_GROUP
    nwb = WINDOW // T + 1
    spb = T // SEL_BLOCK

    @pl.when(qi == 0)
    def _():
        for src, dst in ((ks_ref, ksf), (vs_ref, vsf), (kw_ref, kwf), (vw_ref, vwf)):
            for blk in range(nkb):
                dst[(nkb - 1 - blk) * T:(nkb - blk) * T, :] = src[0, blk * T:(blk + 1) * T, :].astype(BF16)
            dst[nkb * T:, :] = jnp.zeros(((nkb - 1) * T, NSA_HD), BF16)

    for j in range(HG):
        q_scr[j * T:(j + 1) * T, :] = (q_ref[0, :, j * T:(j + 1) * T] * (NSA_HD ** -0.5)).astype(BF16)
    q_all = q_scr[...]
    start = pl.multiple_of((nkb - 1 - qi) * T, T)

    def softmax_pv(s, v):
        m = jnp.max(s, axis=-1, keepdims=True)
        p = jnp.exp(s - m)
        den = jnp.sum(p, axis=-1, keepdims=True)
        o = jnp.dot(p.reshape(HG * T, p.shape[-1]).astype(BF16), v, preferred_element_type=F32)
        return o.reshape(HG, T, v.shape[-1]) / den

    s = _dot_nt(q_all, kc_ref[0, 0]).reshape(HG, T, T) + bc_ref[...]
    m = jnp.max(s, axis=-1, keepdims=True)
    p = jnp.where(s > 0.5 * NEG, jnp.exp(s - m), 0.0)
    den = jnp.sum(p, axis=-1, keepdims=True)
    pn = p / jnp.where(den > 0.0, den, 1.0)
    o_c = _dot(pn.reshape(HG * T, T), vc_ref[0, 0]).reshape(HG, T, NSA_HD)
    psum = pn[0]
    for j in range(1, HG):
        psum = psum + pn[j]

    p_hi = psum.astype(BF16)
    p_lo = (psum - p_hi.astype(F32)).astype(BF16)
    imp = (_dot_nt(ovt_ref[...], p_hi) + _dot_nt(ovt_ref[...], p_lo))[0:n_sblk, :]
    blk = lax.broadcasted_iota(jnp.int32, (n_sblk, T), 0)
    lane = lax.broadcasted_iota(jnp.int32, (n_sblk, T), 1)
    cur = (qi * T + lane) // SEL_BLOCK
    forced = (blk == 0) | (blk == cur) | (blk == cur - 1)
    impm = jnp.where(forced, 1e30, jnp.where(blk <= cur, imp, NEG))
    rank = jnp.zeros((n_sblk, T), F32)
    for mb in range(n_sblk):
        cm = impm[mb:mb + 1, :]
        lower = jnp.where(blk > mb, 1.0, 0.0)
        rank += jnp.where(cm > impm, 1.0, jnp.where(cm == impm, lower, 0.0))
    sel_t = jnp.where(rank < float(min(N_SEL, n_sblk)), 1.0, 0.0)

    rev = jnp.where(blk + lane == spb * qi + spb - 1, 1.0, 0.0)
    sel_back = _dot_tn(sel_t, rev).astype(BF16)

    def selected(ncol):
        keep = jnp.dot(sel_back, e_ref[:, 0:ncol], preferred_element_type=F32)
        s = _dot_nt(q_all, ksf[pl.ds(start, ncol), :]).reshape(HG, T, ncol) + bs_ref[:, :, 0:ncol]
        os_scr[...] = softmax_pv(s + ((keep - 1.0) * -NEG)[None], vsf[pl.ds(start, ncol), :])

    quarter = max(nkb // 4, 1)
    bounds = list(range(quarter, nkb, quarter)) + [nkb]
    for lo_blk, hi_blk in zip([0] + bounds[:-1], bounds):
        pl.when((qi >= lo_blk) & (qi < hi_blk))(functools.partial(selected, hi_blk * T))
    o_s = os_scr[...]

    colw = lax.broadcasted_iota(jnp.int32, (T, nwb * T), 1)
    s = _dot_nt(q_all, kwf[pl.ds(start, nwb * T), :]).reshape(HG, T, nwb * T) + bw_ref[...]
    o_w = softmax_pv(s + jnp.where(colw < (qi + 1) * T, 0.0, NEG)[None], vwf[pl.ds(start, nwb * T), :])

    gates = jax.nn.sigmoid(gate_ref[0])
    for j in range(HG):
        base = 3 * j
        g0 = NSA_GATE_PAD - 3 * NSA_HEADS
        gsel = lambda c: jnp.where(g == 0, gates[:, g0 + c:g0 + c + 1], gates[:, g0 + 3 * HG + c:g0 + 3 * HG + c + 1])
        o = gsel(base) * o_c[j] + gsel(base + 1) * o_s[j] + gsel(base + 2) * o_w[j]
        o_ref[0, :, j * T:(j + 1) * T] = o.astype(o_ref.dtype)


def _bias_kernel(rel_ref, bkt_ref, o_ref):
    h = pl.program_id(0)
    bkt = bkt_ref[...]
    acc = jnp.full(bkt.shape, NEG, F32)
    for b in range(REL_BUCKETS):
        acc = jnp.where(bkt == b, rel_ref[b, h], acc)
    o_ref[0] = acc


def bias_lookup(rel_bias, dist, valid):
    bkt = jnp.where(jnp.asarray(valid), _rel_bucket(jnp.asarray(dist, jnp.int32)), -1)
    R, C = dist.shape
    H = rel_bias.shape[1]
    return pl.pallas_call(
        _bias_kernel,
        out_shape=jax.ShapeDtypeStruct((H, R, C), F32),
        grid=(H,),
        in_specs=[pl.BlockSpec(memory_space=pltpu.SMEM), pl.BlockSpec((R, C), lambda h: (0, 0))],
        out_specs=pl.BlockSpec((1, R, C), lambda h: (h, 0, 0)),
        compiler_params=_cparams(("parallel",)),
        name="bias_lookup",
    )(rel_bias, bkt)


def nsa_tables(rel_bias, S):
    T = LANES
    n_cmp = (S - CMP_BLOCK) // CMP_STRIDE + 1
    n_sblk = S // SEL_BLOCK
    nkb = S // T
    nwb = WINDOW // T + 1
    assert n_cmp < T and n_sblk <= T and S // CMP_STRIDE == T and nwb <= nkb
    ii = np.arange(T)[:, None]
    cc = np.arange(nkb * T)[None, :]
    dist = ii + T * (cc // T) - cc % T
    bias_s = bias_lookup(rel_bias, dist, dist >= 0)
    bias_w = bias_lookup(rel_bias, dist[:, :nwb * T], ((dist >= 0) & (dist < WINDOW))[:, :nwb * T])
    dist_c = np.arange(S)[:, None] - (CMP_STRIDE * np.arange(T) + CMP_BLOCK - 1)[None, :]
    bias_c = bias_lookup(rel_bias, dist_c, (dist_c >= 0) & (np.arange(T)[None, :] < n_cmp))

    cmp_start = CMP_STRIDE * np.arange(T)
    sel_start = SEL_BLOCK * np.arange(T)
    ov = ((cmp_start[:, None] <= (sel_start + SEL_BLOCK - 1)[None, :])
          & ((cmp_start + CMP_BLOCK - 1)[:, None] >= sel_start[None, :])
          & (np.arange(T)[:, None] < n_cmp) & (np.arange(T)[None, :] < n_sblk))
    spb = T // SEL_BLOCK
    c = np.arange(nkb * T)
    back = spb * (c // T) + spb - 1 - (c % T) // SEL_BLOCK
    expand = np.arange(T)[:, None] == back[None, :]
    return bias_c, bias_s, bias_w, jnp.asarray(ov.T, BF16), jnp.asarray(expand, BF16)


def nsa_attention(zm, zt, kcmp, vcmp, tables):
    B, S, _ = zm.shape
    G, HG, d, T = NSA_KV_HEADS, NSA_GROUP, NSA_HD, LANES
    assert G == 2 and d == T
    n_sblk = S // SEL_BLOCK
    nkb = S // T
    nwb = WINDOW // T + 1
    bias_c, bias_s, bias_w, ov_t, expand = tables

    qb = ZM_NSA_Q // (HG * d)
    kvb = ZM_NSA_KV // d
    kvspec = lambda off: pl.BlockSpec((1, S, d), lambda b, g, i: (b, 0, kvb + off * G + g))
    cspec = pl.BlockSpec((1, 1, T, d), lambda b, g, i: (b, g, 0, 0))
    flip = pltpu.VMEM(((2 * nkb - 1) * T, d), BF16)
    return pl.pallas_call(
        functools.partial(_nsa_kernel, n_sblk=n_sblk, nkb=nkb),
        out_shape=jax.ShapeDtypeStruct((B, S, NSA_Q), BF16),
        grid=(B, G, nkb),
        in_specs=[
            pl.BlockSpec((1, T, HG * d), lambda b, g, i: (b, i, qb + g)),
            pl.BlockSpec((1, T, NSA_GATE_PAD), lambda b, g, i: (b, i, ZT_GATE // NSA_GATE_PAD)),
            cspec, cspec, kvspec(2), kvspec(3), kvspec(4), kvspec(5),
            pl.BlockSpec((HG, T, T), lambda b, g, i: (g, i, 0)),
            pl.BlockSpec((HG, T, nkb * T), lambda b, g, i: (g, 0, 0)),
            pl.BlockSpec((HG, T, nwb * T), lambda b, g, i: (g, 0, 0)),
            pl.BlockSpec((T, T), lambda b, g, i: (0, 0)),
            pl.BlockSpec((T, nkb * T), lambda b, g, i: (0, 0)),
        ],
        out_specs=pl.BlockSpec((1, T, HG * d), lambda b, g, i: (b, i, g)),
        scratch_shapes=[pltpu.VMEM((HG * T, d), BF16), pltpu.VMEM((HG, T, d), F32), flip, flip, flip, flip],
        compiler_params=_cparams(("parallel", "parallel", "arbitrary")),
        name="nsa_attention",
    )(zm, zt, kcmp, vcmp, zm, zm, zm, zm, bias_c, bias_s, bias_w, ov_t, expand)


def _rwkv_prep_kernel(zr_ref, zk_ref, zv_ref, zl_ref, mu_ref, mul_ref, vec_ref, w2_ref, a2_ref, g2_ref,
                      r_ref, k_ref, v_ref, lw_ref, kk_ref, kka_ref, gg_ref, c_scr, cl_scr):
    first = pl.program_id(1) == 0
    tb = zr_ref.shape[1]
    row = lax.broadcasted_iota(jnp.int32, (tb, 1), 0)

    def shift(z, mu, carry_ref, slot):
        prev_last = jnp.where(first, 0.0, carry_ref[slot:slot + 1, :])
        prev = jnp.where(row == 0, prev_last, pltpu.roll(z, 1, 0))
        carry_ref[slot:slot + 1, :] = z[tb - 1:tb, :]
        return z + (prev - z) * mu

    r = shift(zr_ref[0], mu_ref[0:1, :], c_scr, 0)
    k = shift(zk_ref[0], mu_ref[1:2, :], c_scr, 1)
    v = shift(zv_ref[0], mu_ref[2:3, :], c_scr, 2)
    xl = shift(zl_ref[0], mul_ref[...], cl_scr, 0)
    w0, a0, k_k, k_a = vec_ref[0:1, :], vec_ref[1:2, :], vec_ref[2:3, :], vec_ref[3:4, :]
    u = -(w0 + _dot(jnp.tanh(xl), w2_ref[...]))
    w_log = -(jnp.maximum(u, 0.0) + jnp.log1p(jnp.exp(-jnp.abs(u)))) - 0.5
    a = jax.nn.sigmoid(a0 + _dot(xl, a2_ref[...]))
    r_ref[0] = r
    k_ref[0] = k * (1.0 + (a - 1.0) * k_a)
    v_ref[0] = v
    lw_ref[0] = -jnp.exp(w_log)
    kk = k * k_k
    kk_ref[0] = kk
    kka_ref[0] = kk * a
    gg_ref[0] = _dot(jax.nn.sigmoid(xl), g2_ref[...])


def rwkv_prep(z3, mu_rkv, mu_lora, vecs, w2p, a2p, g2p, tb):
    B, S, _ = z3.shape
    Dm = RWKV_DIM
    rb = ZT_RWKV // Dm
    zspec = lambda o: pl.BlockSpec((1, tb, Dm), lambda b, i: (b, i, rb + o))
    const = lambda shape: pl.BlockSpec(shape, lambda b, i: (0,) * len(shape))
    ospec = pl.BlockSpec((1, tb, Dm), lambda b, i: (b, i, 0))
    out = jax.ShapeDtypeStruct((B, S, Dm), F32)
    return pl.pallas_call(
        _rwkv_prep_kernel,
        out_shape=(out,) * 7,
        grid=(B, S // tb),
        in_specs=[zspec(0), zspec(1), zspec(2),
                  pl.BlockSpec((1, tb, LORA_PAD), lambda b, i: (b, i, ZT_LORA // LORA_PAD)),
                  const((3, Dm)), const((1, LORA_PAD)), const((4, Dm)),
                  const((LORA_PAD, Dm)), const((LORA_PAD, Dm)), const((LORA_PAD, Dm))],
        out_specs=(ospec,) * 7,
        scratch_shapes=[pltpu.VMEM((8, Dm), F32), pltpu.VMEM((8, LORA_PAD), F32)],
        compiler_params=_cparams(("parallel", "arbitrary")),
        name="rwkv_prep",
    )(z3, z3, z3, z3, mu_rkv, mu_lora, vecs, w2p, a2p, g2p)


def _rwkv_masks():
    T, n = RWKV_CHUNK, RWKV_GW
    idx = np.arange(n)
    h, t = idx // T, idx % T
    same = h[:, None] == h[None, :]
    tt, ss = t[:, None], t[None, :]
    levels = []
    b = 1
    while b < T:
        levels.append(same & (tt // (2 * b) == ss // (2 * b)) & (tt % (2 * b) >= b) & (ss % (2 * b) < b))
        b *= 2
    masks = [same, same & (ss < tt), same & (ss <= tt), np.eye(n, dtype=bool)] + levels
    return np.stack(masks).astype(np.float32), len(levels)


def _rwkv_chunk_kernel(r_ref, k_ref, v_ref, lw_ref, kk_ref, kka_ref, gg_ref, rk_ref, ln_ref, msk_ref, tri_ref,
                       o_ref, s_scr, *, n_levels):
    T, GW = RWKV_CHUNK, RWKV_GW

    @pl.when(pl.program_id(1) == 0)
    def _():
        s_scr[...] = jnp.zeros_like(s_scr)

    m_bd = msk_ref[0]
    m_strict = msk_ref[1]
    m_incl = msk_ref[2]
    eye = msk_ref[3].astype(F32)
    tri = tri_ref[...]

    def to_bd(x):
        return jnp.concatenate([x.astype(BF16)] * RWKV_GROUP, axis=0) * m_bd

    def from_bd(y):
        out = y[0:T]
        for i in range(1, RWKV_GROUP):
            out = out + y[i * T:(i + 1) * T]
        return out

    def seg_sums(xs):
        x = jnp.concatenate(xs, axis=0)
        hi = x.astype(BF16)
        lo = (x - hi.astype(F32)).astype(BF16)
        n = x.shape[0]
        both = mm(jnp.concatenate([hi, lo], axis=0), m_bd)
        out = both[0:n] + both[n:2 * n]
        return [out[i * T:(i + 1) * T] for i in range(len(xs))]

    def mm(a, b):
        return jnp.dot(a, b, preferred_element_type=F32)

    def mm_nt(a, b):
        return lax.dot_general(a, b, (((1,), (1,)), ((), ())), preferred_element_type=F32)

    def mm_tn(a, b):
        return lax.dot_general(a, b, (((0,), (0,)), ((), ())), preferred_element_type=F32)

    groups = range(RWKV_DIM // GW)
    sls = [slice(gi * GW, (gi + 1) * GW) for gi in groups]
    rs = [r_ref[0, :, sl] for sl in sls]
    ks = [k_ref[0, :, sl] for sl in sls]
    vs = [v_ref[0, :, sl] for sl in sls]
    lws = [lw_ref[0, :, sl] for sl in sls]
    sums = seg_sums([kk_ref[0, :, sl] * kk_ref[0, :, sl] for sl in sls]
                    + [r * k * rk_ref[:, sl] for r, k, sl in zip(rs, ks, sls)])
    inv_n = [1.0 / jnp.maximum(jnp.sqrt(ss), 1e-12) for ss in sums[:len(sls)]]
    bonus = [rk_sum * v for rk_sum, v in zip(sums[len(sls):], vs)]
    cum = [_split_dot_left(tri, lw) for lw in lws]
    p_in = [jnp.exp(c) for c in cum]
    p_inv = [jnp.exp(-c) for c in cum]
    a_bd = [to_bd(-(kk_ref[0, :, sl] * n) * jnp.exp(c - lw)) for sl, n, c, lw in zip(sls, inv_n, cum, lws)]
    r_bd = [to_bd(r * p) for r, p in zip(rs, p_in)]
    b_bd = [to_bd(kka_ref[0, :, sl] * n * p) for sl, n, p in zip(sls, inv_n, p_inv)]
    k_bd = [to_bd(k * p) for k, p in zip(ks, p_inv)]
    v_bd = [to_bd(v) for v in vs]

    ar_bd = [jnp.concatenate([a, r], axis=0) for a, r in zip(a_bd, r_bd)]
    on_b = [mm_nt(ar, b) for ar, b in zip(ar_bd, b_bd)]
    on_k = [mm_nt(ar, k) for ar, k in zip(ar_bd, k_bd)]
    a_ab = [(t[0:GW] * m_strict).astype(BF16) for t in on_b]
    a_rb = [(t[GW:2 * GW] * m_incl).astype(BF16) for t in on_b]
    a_k = [jnp.concatenate([(t[0:GW] * m_strict).astype(BF16), (t[GW:2 * GW] * m_incl).astype(BF16)], axis=0)
           for t in on_k]

    x = [eye + a * msk_ref[4] for a in a_ab]
    for lv in range(1, n_levels):
        xb = [xi.astype(BF16) for xi in x]
        t = [mm(xi, a * msk_ref[4 + lv]).astype(BF16) for xi, a in zip(xb, a_ab)]
        x = [xf + mm(ti, xi) for xf, ti, xi in zip(x, t, xb)]
    xb = [xi.astype(BF16) for xi in x]

    state = [s_scr[gi] for gi in groups]
    sb = [s.astype(BF16) for s in state]
    on_s = [mm_nt(ar, s) + mm(ak, v) for ar, s, ak, v in zip(ar_bd, sb, a_k, v_bd)]
    u = [mm(xi, t[0:GW].astype(BF16)).astype(BF16) for xi, t in zip(xb, on_s)]
    y = [from_bd(t[GW:2 * GW] + mm(arb, ui)) for t, arb, ui in zip(on_s, a_rb, u)]
    for gi in groups:
        s_scr[gi] = (state[gi] + mm_tn(u[gi], b_bd[gi]) + mm_tn(v_bd[gi], k_bd[gi])) * p_in[gi][T - 1:T, :]

    mu = [m * (1.0 / RWKV_HD) for m in seg_sums(y)]
    d = [yi - m for yi, m in zip(y, mu)]
    var = [s2 * (1.0 / RWKV_HD) for s2 in seg_sums([di * di for di in d])]
    for gi, sl in enumerate(sls):
        yn = d[gi] * lax.rsqrt(var[gi] + RWKV_LN_EPS) * ln_ref[0:1, sl] + ln_ref[1:2, sl]
        o_ref[0, :, sl] = ((yn + bonus[gi]) * gg_ref[0, :, sl]).astype(o_ref.dtype)


def _split_dot_left(w_bf16, x):
    acc = None
    rem = x
    for _ in range(3):
        hi = rem.astype(BF16)
        t = jnp.dot(w_bf16, hi, preferred_element_type=F32)
        acc = t if acc is None else acc + t
        rem = rem - hi.astype(F32)
    return acc


def rwkv_chunk(r, k, v, lw, kk, kka, gg, rk, ln):
    B, S, Dm = r.shape
    T = RWKV_CHUNK
    masks, n_levels = _rwkv_masks()
    tri = np.tril(np.ones((T, T), np.float32))
    ng = Dm // RWKV_GW
    xspec = pl.BlockSpec((1, T, Dm), lambda b, c: (b, c, 0))
    const = lambda shape: pl.BlockSpec(shape, lambda b, c: (0,) * len(shape))
    return pl.pallas_call(
        functools.partial(_rwkv_chunk_kernel, n_levels=n_levels),
        out_shape=jax.ShapeDtypeStruct((B, S, Dm), BF16),
        grid=(B, S // T),
        in_specs=[xspec] * 7 + [const((1, Dm)), const((2, Dm)), const(masks.shape), const((T, T))],
        out_specs=xspec,
        scratch_shapes=[pltpu.VMEM((ng, RWKV_GW, RWKV_GW), F32)],
        compiler_params=_cparams(("parallel", "arbitrary")),
        name="rwkv_chunk",
    )(r, k, v, lw, kk, kka, gg, rk, ln, jnp.asarray(masks, BF16), jnp.asarray(tri, BF16))


def _cast_kernel(x_ref, o_ref):
    o_ref[...] = x_ref[...].astype(o_ref.dtype)


def cast_leading_cols(w, n_cols, tr, tc):
    L, K, _ = w.shape
    spec = pl.BlockSpec((None, tr, tc), lambda l, i, j: (l, i, j))
    return pl.pallas_call(
        _cast_kernel,
        out_shape=jax.ShapeDtypeStruct((L, K, n_cols), BF16),
        grid=(L, K // tr, n_cols // tc),
        in_specs=[spec],
        out_specs=spec,
        compiler_params=_cparams(("parallel", "parallel", "parallel")),
        name="cast_w_main",
    )(w)


def _tail_cast_kernel(a_ref, b_ref, o_ref, *, shift, n_real):
    j = pl.program_id(1)
    a = a_ref[...]
    b = b_ref[...]
    y = jnp.concatenate([a[:, shift:], b[:, :shift]], axis=1)
    lane = lax.broadcasted_iota(jnp.int32, y.shape, 1)
    y = jnp.where(j * LANES + lane < n_real, y, 0.0)
    lead = jnp.concatenate([jnp.zeros((a.shape[0], LANES - shift), F32), a[:, :shift]], axis=1)
    o_ref[...] = jnp.where(j == pl.num_programs(1) - 1, lead, y).astype(o_ref.dtype)


def cast_tail_cols(w, col0):
    L, K, n_all = w.shape
    shift = 3 * NSA_HEADS
    n_real = n_all - col0 - shift
    nblk = ZT_COLS // LANES
    assert col0 % LANES == 0 and n_real == 3 * RWKV_DIM + LORA_COLS and n_real <= ZT_GATE
    b0 = col0 // LANES
    last = (n_all - 1) // LANES
    return pl.pallas_call(
        functools.partial(_tail_cast_kernel, shift=shift, n_real=n_real),
        out_shape=jax.ShapeDtypeStruct((L, K, ZT_COLS), BF16),
        grid=(L, nblk),
        in_specs=[
            pl.BlockSpec((None, K, LANES), lambda l, j: (l, 0, jnp.where(j == nblk - 1, b0, jnp.minimum(b0 + j, last)))),
            pl.BlockSpec((None, K, LANES), lambda l, j: (l, 0, jnp.minimum(b0 + j + 1, last))),
        ],
        out_specs=pl.BlockSpec((None, K, LANES), lambda l, j: (l, 0, j)),
        compiler_params=_cparams(("parallel", "parallel")),
        name="cast_w_tail",
    )(w, w)


def _prep_weights(w_in, rwkv_mu, rwkv_w2, rwkv_a2, rwkv_g2):
    L, D, _ = w_in.shape
    o = 3 * D + ZM_COLS
    w_main = cast_leading_cols(w_in, o, min(2048, D), 512)
    w_tail = cast_tail_cols(w_in, o)
    mu_rkv = rwkv_mu[:, :3 * RWKV_DIM].reshape(L, 3, RWKV_DIM)
    mu_lora = jnp.pad(rwkv_mu[:, 3 * RWKV_DIM:], ((0, 0), (0, LORA_PAD - LORA_COLS))).reshape(L, 1, LORA_PAD)
    rows = lambda w, start: jnp.pad(w, ((0, 0), (start, LORA_PAD - start - w.shape[1]), (0, 0)))
    w2p = rows(rwkv_w2, 0)
    a2p = rows(rwkv_a2, DECAY_LORA)
    g2p = rows(rwkv_g2, DECAY_LORA + AAA_LORA)
    return w_main, w_tail, mu_rkv, mu_lora, w2p, a2p, g2p


def kernel(x, c, rel_bias, w_in, w_branch_ret, w_branch_nsa, w_branch_rwkv, w_out, ffn1_in, ffn1_out, ffn2_in,
           ffn2_out, ada_down, ada_up, ada_bias, norm_pre, norm_post, cmp_pos, cmp_w1, cmp_b1, cmp_w2, cmp_b2,
           rwkv_mu, rwkv_vecs, rwkv_w2, rwkv_a2, rwkv_g2, rwkv_rk, rwkv_ln):
    B, S, D = x.shape
    L = w_in.shape[0]
    M = B * S
    tm = min(1024, M)
    ts = min(256, S)
    tb = min(256, S)
    tn_gate = _col_tile(3 * D, 1024)
    tn_mix = _col_tile(math.gcd(3 * D, ZM_COLS), 1024)
    tn_tail = _col_tile(ZT_COLS, 512)
    tn_d = _col_tile(D, 1024)

    w_main, w_tail, mu_rkv, mu_lora, w2p, a2p, g2p = _prep_weights(w_in, rwkv_mu, rwkv_w2, rwkv_a2, rwkv_g2)
    tables = nsa_tables(rel_bias, S)
    w_bret, w_bnsa, w_brwkv = w_branch_ret.astype(BF16), w_branch_nsa.astype(BF16), w_branch_rwkv.astype(BF16)
    w_o = w_out.astype(BF16)
    f1i, f1o, f2i, f2o = (w.astype(BF16) for w in (ffn1_in, ffn1_out, ffn2_in, ffn2_out))
    npre = norm_pre.reshape(L, N_SUB, 1, D)
    npost = norm_post.reshape(L, N_SUB, 1, D)
    rk = rwkv_rk.reshape(L, 1, RWKV_DIM)

    mod = ada_mod(c, ada_down, ada_up, ada_bias)
    h = norm_modulate(x, npre, mod, 0, 0, ts)
    for l in range(L):
        u = swiglu_in(h.reshape(M, D), f1i, l, tm, 512)
        y = matmul(u, f1o, l, tm, tn_d, F32, name="ffn_out")
        x, h = post_residual(x, y.reshape(B, S, D), npost, npre, mod, l, 0, 0.5, (l, 1), ts)
        h2 = h.reshape(M, D)
        zg = matmul(h2, w_main, l, tm, tn_gate, BF16, act="sigmoid", name="gate_proj", n_cols=3 * D)
        zm = matmul(h2, w_main, l, tm, tn_mix, F32, name="mix_proj", col0=3 * D, n_cols=ZM_COLS)
        zm = zm.reshape(B, S, ZM_COLS)
        zt = matmul(h2, w_tail, l, tm, tn_tail, F32, name="tail_proj").reshape(B, S, ZT_COLS)
        o_ret = retention(zm)
        kcmp, vcmp = nsa_compress(zm, cmp_pos[l], cmp_w1[l], cmp_b1[l], cmp_w2[l], cmp_b2[l])
        o_nsa = nsa_attention(zm, zt, kcmp, vcmp, tables)
        rw = rwkv_prep(zt, mu_rkv[l], mu_lora[l], rwkv_vecs[l], w2p[l], a2p[l], g2p[l], tb)
        o_rwkv = rwkv_chunk(*rw, rk[l], rwkv_ln[l])
        merged = branch_merge(o_ret.reshape(M, RET_V), o_nsa.reshape(M, NSA_Q), o_rwkv.reshape(M, RWKV_DIM),
                              w_bret, w_bnsa, w_brwkv, zg, l, tm, 512)
        y = matmul(merged, w_o, l, tm, tn_d, F32, name="out_proj")
        x, h = post_residual(x, y.reshape(B, S, D), npost, npre, mod, l, 1, 1.0, (l, 2), ts)
        u = swiglu_in(h.reshape(M, D), f2i, l, tm, 512)
        y = matmul(u, f2o, l, tm, tn_d, F32, name="ffn_out")
        nxt = (l + 1, 0) if l + 1 < L else None
        x, h = post_residual(x, y.reshape(B, S, D), npost, npre, mod, l, 2, 0.5, nxt, ts)
    return x
```

```python
import functools
import math

import numpy as np
import jax
import jax.numpy as jnp
from jax import lax
from jax.experimental import pallas as pl
from jax.experimental.pallas import tpu as pltpu

F32 = jnp.float32
BF16 = jnp.bfloat16

D_MODEL = 4096
DEPTH = 4
D_FF = 3072
N_SUB = 3
RET_HEADS, RET_QK_HD, RET_V_HD, RET_CHUNK = 8, 128, 256, 128
RET_QK = RET_HEADS * RET_QK_HD
RET_V = RET_HEADS * RET_V_HD
ROPE_BASE = 10000.0
NSA_HEADS, NSA_KV_HEADS, NSA_HD = 8, 2, 128
NSA_GROUP = NSA_HEADS // NSA_KV_HEADS
NSA_Q = NSA_HEADS * NSA_HD
NSA_KV = NSA_KV_HEADS * NSA_HD
CMP_BLOCK, CMP_STRIDE = 32, 16
SEL_BLOCK, N_SEL = 64, 8
WINDOW = 512
RWKV_HD, RWKV_DIM = 64, 1024
RWKV_HEADS = RWKV_DIM // RWKV_HD
DECAY_LORA, AAA_LORA, GATE_LORA = 64, 64, 160
RWKV_LN_EPS = 64e-5
REL_BUCKETS, REL_MAX_DIST = 32, 128
RET_COLS = 2 * RET_QK + 2 * RET_V
NSA_COLS = NSA_Q + 6 * NSA_KV + 3 * NSA_HEADS
RWKV_COLS = 3 * RWKV_DIM + DECAY_LORA + AAA_LORA + GATE_LORA
NEG = -1e30

LANES = 128
VMEM_LIMIT = 56 * 1024 * 1024

ZM_RET = 0
ZM_NSA_Q = RET_COLS
ZM_NSA_KV = ZM_NSA_Q + NSA_Q
ZM_COLS = ZM_NSA_KV + 6 * NSA_KV
ZT_RWKV = 0
ZT_LORA = 3 * RWKV_DIM
LORA_COLS = DECAY_LORA + AAA_LORA + GATE_LORA
LORA_PAD = 3 * LANES
ZT_GATE = ZT_LORA + LORA_PAD
NSA_GATE_PAD = LANES
ZT_COLS = ZT_GATE + NSA_GATE_PAD

RWKV_CHUNK = 64
RWKV_GROUP = 4
RWKV_GW = RWKV_GROUP * RWKV_HD


def _col_tile(n, pref):
    return max(t for t in range(LANES, min(n, pref) + 1, LANES) if n % t == 0)


def _cparams(sem):
    return pltpu.CompilerParams(dimension_semantics=sem, vmem_limit_bytes=VMEM_LIMIT)


def _dot(a, b):
    return jnp.dot(a.astype(BF16), b.astype(BF16), preferred_element_type=F32)


def _dot_nt(a, b):
    return lax.dot_general(a.astype(BF16), b.astype(BF16), (((1,), (1,)), ((), ())),
                           preferred_element_type=F32)


def _dot_tn(a, b):
    return lax.dot_general(a.astype(BF16), b.astype(BF16), (((0,), (0,)), ((), ())),
                           preferred_element_type=F32)


def _split_dot(x, w_bf16, parts):
    acc = None
    rem = x
    for _ in range(parts):
        hi = rem.astype(BF16)
        t = jnp.dot(hi, w_bf16, preferred_element_type=F32)
        acc = t if acc is None else acc + t
        rem = rem - hi.astype(F32)
    return acc


def _silu(x):
    return x * jax.nn.sigmoid(x)


def _ada_kernel(c_ref, down_ref, up_ref, bias_ref, o_ref):
    t = _dot(_silu(c_ref[...]), down_ref[...])
    o_ref[...] = _dot(t, up_ref[...]) + bias_ref[...]


def ada_mod(c, ada_down, ada_up, ada_bias):
    L, D, R = ada_down.shape
    B = c.shape[0]
    N = ada_up.shape[-1]
    tn = D
    out = pl.pallas_call(
        _ada_kernel,
        out_shape=jax.ShapeDtypeStruct((L, B, N), F32),
        grid=(L, N // tn),
        in_specs=[
            pl.BlockSpec((B, D), lambda l, j: (0, 0)),
            pl.BlockSpec((None, D, R), lambda l, j: (l, 0, 0)),
            pl.BlockSpec((None, R, tn), lambda l, j: (l, 0, j)),
            pl.BlockSpec((None, 1, tn), lambda l, j: (l, 0, j)),
        ],
        out_specs=pl.BlockSpec((None, B, tn), lambda l, j: (l, 0, j)),
        compiler_params=_cparams(("parallel", "parallel")),
        name="ada_mod",
    )(c, ada_down, ada_up, ada_bias.reshape(L, 1, N))
    return out.reshape(L, B, 3 * N_SUB, D)


def _rms(x, g):
    return x * lax.rsqrt(jnp.mean(x * x, axis=-1, keepdims=True) + 1e-6) * g


def _normmod_kernel(x_ref, g_ref, mod_ref, h_ref, *, sub):
    shift = mod_ref[3 * sub:3 * sub + 1, :]
    scale = mod_ref[3 * sub + 1:3 * sub + 2, :]
    h_ref[0] = (_rms(x_ref[0], g_ref[...]) * (1.0 + scale) + shift).astype(h_ref.dtype)


def norm_modulate(x, gain, mod, l, sub, ts):
    B, S, D = x.shape
    return pl.pallas_call(
        functools.partial(_normmod_kernel, sub=sub),
        out_shape=jax.ShapeDtypeStruct((B, S, D), BF16),
        grid=(B, S // ts),
        in_specs=[
            pl.BlockSpec((1, ts, D), lambda b, i: (b, i, 0)),
            pl.BlockSpec((None, None, 1, D), lambda b, i: (l, sub, 0, 0)),
            pl.BlockSpec((None, None, 3 * N_SUB, D), lambda b, i: (l, b, 0, 0)),
        ],
        out_specs=pl.BlockSpec((1, ts, D), lambda b, i: (b, i, 0)),
        compiler_params=_cparams(("parallel", "parallel")),
        name="norm_modulate",
    )(x, gain, mod)


def _post_kernel(x_ref, y_ref, gpost_ref, mod_ref, gpre_ref, modn_ref, xo_ref, h_ref, *, sub, coef, nsub):
    gate = mod_ref[3 * sub + 2:3 * sub + 3, :]
    xn = x_ref[0] + coef * (gate * _rms(y_ref[0], gpost_ref[...]))
    xo_ref[0] = xn
    shift = modn_ref[3 * nsub:3 * nsub + 1, :]
    scale = modn_ref[3 * nsub + 1:3 * nsub + 2, :]
    h_ref[0] = (_rms(xn, gpre_ref[...]) * (1.0 + scale) + shift).astype(h_ref.dtype)


def _post_last_kernel(x_ref, y_ref, gpost_ref, mod_ref, xo_ref, *, sub, coef):
    gate = mod_ref[3 * sub + 2:3 * sub + 3, :]
    xo_ref[0] = x_ref[0] + coef * (gate * _rms(y_ref[0], gpost_ref[...]))


def post_residual(x, y, norm_post, norm_pre, mod, l, sub, coef, nxt, ts):
    B, S, D = x.shape
    xspec = pl.BlockSpec((1, ts, D), lambda b, i: (b, i, 0))
    gspec = lambda ll, ss: pl.BlockSpec((None, None, 1, D), lambda b, i: (ll, ss, 0, 0))
    mspec = lambda ll: pl.BlockSpec((None, None, 3 * N_SUB, D), lambda b, i: (ll, b, 0, 0))
    if nxt is None:
        return pl.pallas_call(
            functools.partial(_post_last_kernel, sub=sub, coef=coef),
            out_shape=jax.ShapeDtypeStruct((B, S, D), F32),
            grid=(B, S // ts),
            in_specs=[xspec, xspec, gspec(l, sub), mspec(l)],
            out_specs=xspec,
            compiler_params=_cparams(("parallel", "parallel")),
            name="post_last",
        )(x, y, norm_post, mod), None
    l2, sub2 = nxt
    return pl.pallas_call(
        functools.partial(_post_kernel, sub=sub, coef=coef, nsub=sub2),
        out_shape=(jax.ShapeDtypeStruct((B, S, D), F32), jax.ShapeDtypeStruct((B, S, D), BF16)),
        grid=(B, S // ts),
        in_specs=[xspec, xspec, gspec(l, sub), mspec(l), gspec(l2, sub2), mspec(l2)],
        out_specs=(xspec, xspec),
        compiler_params=_cparams(("parallel", "parallel")),
        name="post_residual",
    )(x, y, norm_post, mod, norm_pre, mod)


def _mm_kernel(a_ref, w_ref, o_ref, *, act, w_rows_are_outputs):
    if w_rows_are_outputs:
        acc = lax.dot_general(a_ref[...], w_ref[...], (((1,), (1,)), ((), ())), preferred_element_type=F32)
    else:
        acc = jnp.dot(a_ref[...], w_ref[...], preferred_element_type=F32)
    if act == "sigmoid":
        acc = jax.nn.sigmoid(acc)
    o_ref[...] = acc.astype(o_ref.dtype)


def matmul(a, w, l, tm, tn, out_dtype, act=None, name="matmul", col0=0, n_cols=None, w_rows_are_outputs=False):
    M, K = a.shape
    n_all = w.shape[1] if w_rows_are_outputs else w.shape[2]
    N = n_all if n_cols is None else n_cols
    assert col0 % tn == 0 and N % tn == 0
    j0 = col0 // tn
    if w_rows_are_outputs:
        wspec = pl.BlockSpec((None, tn, K), lambda i, j: (l, j0 + j, 0))
    else:
        wspec = pl.BlockSpec((None, K, tn), lambda i, j: (l, 0, j0 + j))
    return pl.pallas_call(
        functools.partial(_mm_kernel, act=act, w_rows_are_outputs=w_rows_are_outputs),
        out_shape=jax.ShapeDtypeStruct((M, N), out_dtype),
        grid=(M // tm, N // tn),
        in_specs=[pl.BlockSpec((tm, K), lambda i, j: (i, 0)), wspec],
        out_specs=pl.BlockSpec((tm, tn), lambda i, j: (i, j)),
        compiler_params=_cparams(("parallel", "arbitrary")),
        name=name,
    )(a, w)


def _swiglu_kernel(a_ref, wa_ref, wb_ref, o_ref):
    h = a_ref[...]
    a = jnp.dot(h, wa_ref[...], preferred_element_type=F32)
    b = jnp.dot(h, wb_ref[...], preferred_element_type=F32)
    o_ref[...] = (_silu(a) * b).astype(o_ref.dtype)


def swiglu_in(h, w, l, tm, tn):
    M, K = h.shape
    F = w.shape[-1] // 2
    nb = F // tn
    return pl.pallas_call(
        _swiglu_kernel,
        out_shape=jax.ShapeDtypeStruct((M, F), BF16),
        grid=(M // tm, nb),
        in_specs=[
            pl.BlockSpec((tm, K), lambda i, j: (i, 0)),
            pl.BlockSpec((None, K, tn), lambda i, j: (l, 0, j)),
            pl.BlockSpec((None, K, tn), lambda i, j: (l, 0, j + nb)),
        ],
        out_specs=pl.BlockSpec((tm, tn), lambda i, j: (i, j)),
        compiler_params=_cparams(("parallel", "arbitrary")),
        name="swiglu_in",
    )(h, w, w)


def _merge_kernel(o1_ref, o2_ref, o3_ref, w1_ref, w2_ref, w3_ref, g1_ref, g2_ref, g3_ref, o_ref):
    acc = g1_ref[...].astype(F32) * jnp.dot(o1_ref[...], w1_ref[...], preferred_element_type=F32)
    acc += g2_ref[...].astype(F32) * jnp.dot(o2_ref[...], w2_ref[...], preferred_element_type=F32)
    acc += g3_ref[...].astype(F32) * jnp.dot(o3_ref[...], w3_ref[...], preferred_element_type=F32)
    o_ref[...] = acc.astype(o_ref.dtype)


def branch_merge(o_ret, o_nsa, o_rwkv, w_ret, w_nsa, w_rwkv, zg, l, tm, tn):
    M = o_ret.shape[0]
    D = w_ret.shape[-1]
    nb = D // tn
    ospec = lambda o: pl.BlockSpec((tm, o.shape[1]), lambda i, j: (i, 0))
    wspec = lambda w: pl.BlockSpec((None, w.shape[1], tn), lambda i, j: (l, 0, j))
    gspec = lambda k: pl.BlockSpec((tm, tn), lambda i, j: (i, j + k * nb))
    return pl.pallas_call(
        _merge_kernel,
        out_shape=jax.ShapeDtypeStruct((M, D), BF16),
        grid=(M // tm, nb),
        in_specs=[ospec(o_ret), ospec(o_nsa), ospec(o_rwkv), wspec(w_ret), wspec(w_nsa), wspec(w_rwkv),
                  gspec(0), gspec(1), gspec(2)],
        out_specs=pl.BlockSpec((tm, tn), lambda i, j: (i, j)),
        compiler_params=_cparams(("parallel", "arbitrary")),
        name="branch_merge",
    )(o_ret, o_nsa, o_rwkv, w_ret, w_nsa, w_rwkv, zg, zg, zg)


POST_RING = 3


def _mm_post_kernel(*refs, sub, coef, nsub, n_i, n_j, slab, has_next):
    if has_next:
        (a_ref, w_ref, gpost_ref, mod_ref, gpre_ref, modn_ref, x_hbm, xo_hbm, h_hbm,
         y_scr, xbuf, xobuf, hbuf, sem_x, sem_xo, sem_h) = refs
    else:
        a_ref, w_ref, gpost_ref, mod_ref, x_hbm, xo_hbm, y_scr, xbuf, xobuf, sem_x, sem_xo = refs
    i = pl.program_id(0)
    j = pl.program_id(1)
    step = i * n_j + j
    e = step - n_j
    n_slabs = n_i * n_j
    ahead = POST_RING - 1
    tn = w_ref.shape[-1]
    D = n_j * tn

    def slot(s):
        return (s + POST_RING * n_j) % POST_RING

    def x_copy(s):
        return pltpu.make_async_copy(x_hbm.at[pl.ds(s * slab, slab), :], xbuf.at[slot(s)], sem_x.at[slot(s)])

    def xo_copy(s):
        return pltpu.make_async_copy(xobuf.at[slot(s)], xo_hbm.at[pl.ds(s * slab, slab), :], sem_xo.at[slot(s)])

    def h_copy(s):
        return pltpu.make_async_copy(hbuf.at[slot(s)], h_hbm.at[pl.ds(s * slab, slab), :], sem_h.at[slot(s)])

    @pl.when(step == 0)
    def _():
        y_scr[...] = jnp.zeros_like(y_scr)
        xbuf[...] = jnp.zeros_like(xbuf)

    @pl.when(e >= 0)
    def _():
        x_copy(e).wait()

    @pl.when(e >= POST_RING)
    def _():
        xo_copy(e - POST_RING).wait()
        if has_next:
            h_copy(e - POST_RING).wait()

    y_scr[i % 2, j] = jnp.dot(a_ref[...], w_ref[...], preferred_element_type=F32)

    prev = (i + 1) % 2
    sl = slot(e)
    r0 = pl.multiple_of(j * slab, slab)
    cols = [slice(c * tn, (c + 1) * tn) for c in range(n_j)]
    ys = [y_scr[prev, c, pl.ds(r0, slab), :] for c in range(n_j)]
    ss = ys[0] * ys[0]
    for y in ys[1:]:
        ss = ss + y * y
    inv = lax.rsqrt(jnp.sum(ss, axis=-1, keepdims=True) * (1.0 / D) + 1e-6)
    ss2 = None
    for c, cs in enumerate(cols):
        gate = mod_ref[3 * sub + 2:3 * sub + 3, cs]
        xn = xbuf[sl, :, cs] + coef * (gate * (ys[c] * inv * gpost_ref[:, cs]))
        xobuf[sl, :, cs] = xn
        ss2 = xn * xn if ss2 is None else ss2 + xn * xn
    if has_next:
        inv2 = lax.rsqrt(jnp.sum(ss2, axis=-1, keepdims=True) * (1.0 / D) + 1e-6)
        for cs in cols:
            shift = modn_ref[3 * nsub:3 * nsub + 1, cs]
            scale = modn_ref[3 * nsub + 1:3 * nsub + 2, cs]
            hbuf[sl, :, cs] = (xobuf[sl, :, cs] * inv2 * gpre_ref[:, cs] * (1.0 + scale) + shift).astype(hbuf.dtype)

    @pl.when(e >= 0)
    def _():
        xo_copy(e).start()
        if has_next:
            h_copy(e).start()

    @pl.when((e + ahead >= 0) & (e + ahead < n_slabs))
    def _():
        x_copy(e + ahead).start()

    @pl.when(e == n_slabs - 1)
    def _():
        for back in range(POST_RING):
            xo_copy(e - back).wait()
            if has_next:
                h_copy(e - back).wait()


def matmul_post(a, w, x, norm_post, norm_pre, mod, l, sub, coef, nxt, tm, tn):
    M, K = a.shape
    B, S, D = x.shape
    n_i, n_j = M // tm, D // tn
    slab = tm // n_j
    assert tm * n_i == M and tn * n_j == D and slab * n_j == tm and slab % 16 == 0 and S % tm == 0
    assert n_j >= POST_RING - 1 and n_i * n_j >= POST_RING
    has_next = nxt is not None
    batch = lambda i: (jnp.maximum(i - 1, 0) * tm) // S
    gspec = lambda ll, ss: pl.BlockSpec((None, None, 1, D), lambda i, j: (ll, ss, 0, 0))
    mspec = lambda ll: pl.BlockSpec((None, None, 3 * N_SUB, D), lambda i, j: (ll, batch(i), 0, 0))
    hbm = pl.BlockSpec(memory_space=pl.ANY)
    in_specs = [pl.BlockSpec((tm, K), lambda i, j: (jnp.minimum(i, n_i - 1), 0)),
                pl.BlockSpec((None, K, tn), lambda i, j: (l, 0, j)),
                gspec(l, sub), mspec(l)]
    args = [a, w, norm_post, mod]
    out_shape = [jax.ShapeDtypeStruct((M, D), F32)]
    scratch = [pltpu.VMEM((2, n_j, tm, tn), F32), pltpu.VMEM((POST_RING, slab, D), F32),
               pltpu.VMEM((POST_RING, slab, D), F32)]
    sems = [pltpu.SemaphoreType.DMA((POST_RING,)), pltpu.SemaphoreType.DMA((POST_RING,))]
    nsub = 0
    if has_next:
        l2, nsub = nxt
        in_specs += [gspec(l2, nsub), mspec(l2)]
        args += [norm_pre, mod]
        out_shape.append(jax.ShapeDtypeStruct((M, D), BF16))
        scratch.append(pltpu.VMEM((POST_RING, slab, D), BF16))
        sems.append(pltpu.SemaphoreType.DMA((POST_RING,)))
    out = pl.pallas_call(
        functools.partial(_mm_post_kernel, sub=sub, coef=coef, nsub=nsub, n_i=n_i, n_j=n_j, slab=slab,
                          has_next=has_next),
        out_shape=tuple(out_shape),
        grid=(n_i + 1, n_j),
        in_specs=in_specs + [hbm],
        out_specs=tuple(hbm for _ in out_shape),
        scratch_shapes=scratch + sems,
        compiler_params=_cparams(("arbitrary", "arbitrary")),
        name="matmul_post",
    )(*args, x.reshape(M, D))
    x_new = out[0].reshape(B, S, D)
    return (x_new, out[1].reshape(B, S, D)) if has_next else (x_new, None)


def _ret_kernel(q_ref, k_ref, v_ref, g_ref, cos_ref, sin_ref, dm_ref, zeta_ref, xi_ref, o_ref, r_scr, *, decays):
    @pl.when(pl.program_id(1) == 0)
    def _():
        r_scr[...] = jnp.zeros_like(r_scr)

    cos = cos_ref[...]
    sin = sin_ref[...]
    dk, dv = RET_QK_HD, RET_V_HD
    for h in range(RET_HEADS):
        qh = q_ref[0, :, h * dk:(h + 1) * dk]
        kh = k_ref[0, :, h * dk:(h + 1) * dk]
        qh = qh * cos + pltpu.roll(qh, dk // 2, 1) * sin
        kh = (kh * cos + pltpu.roll(kh, dk // 2, 1) * sin) * (dk ** -0.5)
        vh = v_ref[0, :, h * dv:(h + 1) * dv]
        s = _dot_nt(qh, kh) * dm_ref[h]
        state = r_scr[h]
        o = _dot(s, vh) + _dot(qh, state) * xi_ref[h]
        mu = jnp.mean(o, axis=-1, keepdims=True)
        d = o - mu
        var = jnp.mean(d * d, axis=-1, keepdims=True)
        on = d * lax.rsqrt(var + 1e-6)
        gh = g_ref[0, :, h * dv:(h + 1) * dv]
        o_ref[0, :, h * dv:(h + 1) * dv] = (_silu(gh) * on).astype(o_ref.dtype)
        r_scr[h] = _dot_tn(kh * zeta_ref[h], vh) + decays[h] * state


def retention(z3):
    B, S, _ = z3.shape
    H, C = RET_HEADS, RET_CHUNK
    pos = jnp.arange(S, dtype=F32)
    inv = 1.0 / (ROPE_BASE ** jnp.linspace(0.0, 1.0, RET_QK_HD // 2))
    ang = pos[:, None] * inv[None, :]
    cos, sin = jnp.cos(ang), jnp.sin(ang)
    cos_f = jnp.concatenate([cos, cos], axis=-1)
    sin_f = jnp.concatenate([-sin, sin], axis=-1)
    log_g = jnp.log1p(-(2.0 ** (-5.0 - jnp.arange(H, dtype=F32))))
    idx = jnp.arange(C, dtype=F32)
    diff = idx[:, None] - idx[None, :]
    dmask = jnp.where(diff >= 0, jnp.exp(jnp.maximum(diff, 0.0)[None] * log_g[:, None, None]), 0.0)
    zeta = jnp.exp((C - 1 - idx)[None, :] * log_g[:, None])
    xi = jnp.exp((idx + 1)[None, :] * log_g[:, None])
    zeta_t = jnp.broadcast_to(zeta[:, :, None], (H, C, RET_QK_HD))
    xi_t = jnp.broadcast_to(xi[:, :, None], (H, C, RET_V_HD))
    lg64 = np.log1p(-(2.0 ** (-5.0 - np.arange(H, dtype=np.float64))))
    decays = tuple(float(v) for v in np.exp(C * lg64))
    qb = ZM_RET // RET_QK
    vb = (ZM_RET + 2 * RET_QK) // RET_V
    const = lambda shape: pl.BlockSpec(shape, lambda b, c: (0,) * len(shape))
    return pl.pallas_call(
        functools.partial(_ret_kernel, decays=decays),
        out_shape=jax.ShapeDtypeStruct((B, S, RET_V), BF16),
        grid=(B, S // C),
        in_specs=[
            pl.BlockSpec((1, C, RET_QK), lambda b, c: (b, c, qb)),
            pl.BlockSpec((1, C, RET_QK), lambda b, c: (b, c, qb + 1)),
            pl.BlockSpec((1, C, RET_V), lambda b, c: (b, c, vb)),
            pl.BlockSpec((1, C, RET_V), lambda b, c: (b, c, vb + 1)),
            pl.BlockSpec((C, RET_QK_HD), lambda b, c: (c, 0)),
            pl.BlockSpec((C, RET_QK_HD), lambda b, c: (c, 0)),
            const((H, C, C)), const((H, C, RET_QK_HD)), const((H, C, RET_V_HD)),
        ],
        out_specs=pl.BlockSpec((1, C, RET_V), lambda b, c: (b, c, 0)),
        scratch_shapes=[pltpu.VMEM((H, RET_QK_HD, RET_V_HD), F32)],
        compiler_params=_cparams(("parallel", "arbitrary")),
        name="retention",
    )(z3, z3, z3, z3, cos_f, sin_f, dmask, zeta_t, xi_t)


def _rel_bucket(dist):
    n = jnp.maximum(dist, 0)
    max_exact = REL_BUCKETS // 2
    nf = jnp.maximum(n, 1).astype(F32)
    large = max_exact + (jnp.log(nf / max_exact) / math.log(REL_MAX_DIST / max_exact)
                         * (REL_BUCKETS - max_exact)).astype(jnp.int32)
    large = jnp.minimum(large, REL_BUCKETS - 1)
    return jnp.where(n < max_exact, n, large)


def _cmp_kernel(kc_ref, vc_ref, pos_ref, w1_ref, b1_ref, w2_ref, b2_ref, ko_ref, vo_ref, *, nb):
    d = NSA_HD
    half = CMP_BLOCK // 2
    for i, (src, dst) in enumerate(((kc_ref, ko_ref), (vc_ref, vo_ref))):
        p1 = jnp.zeros((nb, w1_ref.shape[-1]), F32)
        p2 = jnp.zeros((nb, w1_ref.shape[-1]), F32)
        for t in range(half):
            a = src[0, pl.ds(t, nb, stride=CMP_STRIDE), :]
            p1 += _dot(a + pos_ref[i, t:t + 1, :], w1_ref[i, t * d:(t + 1) * d, :])
            p2 += _dot(a + pos_ref[i, half + t:half + t + 1, :], w1_ref[i, (half + t) * d:(half + t + 1) * d, :])
        pre = p1 + pltpu.roll(p2, nb - 1, 0) + b1_ref[i]
        dst[0, 0] = _dot(jax.nn.gelu(pre), w2_ref[i]) + b2_ref[i]


def nsa_compress(z3, cmp_pos, cmp_w1, cmp_b1, cmp_w2, cmp_b2):
    B, S, _ = z3.shape
    G, d = NSA_KV_HEADS, NSA_HD
    nb = S // CMP_STRIDE
    kb = ZM_NSA_KV // d
    hid = cmp_w1.shape[-1]
    const = lambda shape: pl.BlockSpec(shape, lambda b, g: (0,) * len(shape))
    out = jax.ShapeDtypeStruct((B, G, nb, d), F32)
    return pl.pallas_call(
        functools.partial(_cmp_kernel, nb=nb),
        out_shape=(out, out),
        grid=(B, G),
        in_specs=[
            pl.BlockSpec((1, S, d), lambda b, g: (b, 0, kb + g)),
            pl.BlockSpec((1, S, d), lambda b, g: (b, 0, kb + G + g)),
            const((2, CMP_BLOCK, d)), const((2, CMP_BLOCK * d, hid)), const((2, 1, hid)),
            const((2, hid, d)), const((2, 1, d)),
        ],
        out_specs=(pl.BlockSpec((1, 1, nb, d), lambda b, g: (b, g, 0, 0)),
                   pl.BlockSpec((1, 1, nb, d), lambda b, g: (b, g, 0, 0))),
        compiler_params=_cparams(("parallel", "parallel")),
        name="nsa_compress",
    )(z3, z3, cmp_pos, cmp_w1, cmp_b1.reshape(2, 1, hid), cmp_w2, cmp_b2.reshape(2, 1, d))


def _nsa_kernel(q_ref, gate_ref, kc_ref, vc_ref, ks_ref, vs_ref, kw_ref, vw_ref, bc_ref, bs_ref, bw_ref, ovt_ref, e_ref,
                o_ref, q_scr, os_scr, ksf, vsf, kwf, vwf, *, n_sblk, nkb):
    g = pl.program_id(1)
    qi = pl.program_id(2)
    T = LANES
    HG = NSA_GROUP
    nwb = WINDOW // T + 1
    spb = T // SEL_BLOCK

    @pl.when(qi == 0)
    def _():
        for src, dst in ((ks_ref, ksf), (vs_ref, vsf), (kw_ref, kwf), (vw_ref, vwf)):
            for blk in range(nkb):
                dst[(nkb - 1 - blk) * T:(nkb - blk) * T, :] = src[0, blk * T:(blk + 1) * T, :].astype(BF16)
            dst[nkb * T:, :] = jnp.zeros(((nkb - 1) * T, NSA_HD), BF16)

    for j in range(HG):
        q_scr[j * T:(j + 1) * T, :] = (q_ref[0, :, j * T:(j + 1) * T] * (NSA_HD ** -0.5)).astype(BF16)
    q_all = q_scr[...]
    start = pl.multiple_of((nkb - 1 - qi) * T, T)

    def softmax_pv(s, v):
        m = jnp.max(s, axis=-1, keepdims=True)
        p = jnp.exp(s - m)
        den = jnp.sum(p, axis=-1, keepdims=True)
        o = jnp.dot(p.reshape(HG * T, p.shape[-1]).astype(BF16), v, preferred_element_type=F32)
        return o.reshape(HG, T, v.shape[-1]) / den

    s = _dot_nt(q_all, kc_ref[0, 0]).reshape(HG, T, T) + bc_ref[...]
    m = jnp.max(s, axis=-1, keepdims=True)
    p = jnp.where(s > 0.5 * NEG, jnp.exp(s - m), 0.0)
    den = jnp.sum(p, axis=-1, keepdims=True)
    pn = p / jnp.where(den > 0.0, den, 1.0)
    o_c = _dot(pn.reshape(HG * T, T), vc_ref[0, 0]).reshape(HG, T, NSA_HD)
    psum = pn[0]
    for j in range(1, HG):
        psum = psum + pn[j]

    p_hi = psum.astype(BF16)
    p_lo = (psum - p_hi.astype(F32)).astype(BF16)
    imp = (_dot_nt(ovt_ref[...], p_hi) + _dot_nt(ovt_ref[...], p_lo))[0:n_sblk, :]
    blk = lax.broadcasted_iota(jnp.int32, (n_sblk, T), 0)
    lane = lax.broadcasted_iota(jnp.int32, (n_sblk, T), 1)
    cur = (qi * T + lane) // SEL_BLOCK
    forced = (blk == 0) | (blk == cur) | (blk == cur - 1)
    impm = jnp.where(forced, 1e30, jnp.where(blk <= cur, imp, NEG))
    rank = jnp.zeros((n_sblk, T), F32)
    for mb in range(n_sblk):
        cm = impm[mb:mb + 1, :]
        lower = jnp.where(blk > mb, 1.0, 0.0)
        rank += jnp.where(cm > impm, 1.0, jnp.where(cm == impm, lower, 0.0))
    sel_t = jnp.where(rank < float(min(N_SEL, n_sblk)), 1.0, 0.0)

    rev = jnp.where(blk + lane == spb * qi + spb - 1, 1.0, 0.0)
    sel_back = _dot_tn(sel_t, rev).astype(BF16)

    def selected(ncol):
        keep = jnp.dot(sel_back, e_ref[:, 0:ncol], preferred_element_type=F32)
        s = _dot_nt(q_all, ksf[pl.ds(start, ncol), :]).reshape(HG, T, ncol) + bs_ref[:, :, 0:ncol]
        os_scr[...] = softmax_pv(s + ((keep - 1.0) * -NEG)[None], vsf[pl.ds(start, ncol), :])

    quarter = max(nkb // 4, 1)
    bounds = list(range(quarter, nkb, quarter)) + [nkb]
    for lo_blk, hi_blk in zip([0] + bounds[:-1], bounds):
        pl.when((qi >= lo_blk) & (qi < hi_blk))(functools.partial(selected, hi_blk * T))
    o_s = os_scr[...]

    colw = lax.broadcasted_iota(jnp.int32, (T, nwb * T), 1)
    s = _dot_nt(q_all, kwf[pl.ds(start, nwb * T), :]).reshape(HG, T, nwb * T) + bw_ref[...]
    o_w = softmax_pv(s + jnp.where(colw < (qi + 1) * T, 0.0, NEG)[None], vwf[pl.ds(start, nwb * T), :])

    gates = jax.nn.sigmoid(gate_ref[0])
    for j in range(HG):
        base = 3 * j
        g0 = NSA_GATE_PAD - 3 * NSA_HEADS
        gsel = lambda c: jnp.where(g == 0, gates[:, g0 + c:g0 + c + 1], gates[:, g0 + 3 * HG + c:g0 + 3 * HG + c + 1])
        o = gsel(base) * o_c[j] + gsel(base + 1) * o_s[j] + gsel(base + 2) * o_w[j]
        o_ref[0, :, j * T:(j + 1) * T] = o.astype(o_ref.dtype)


def _bias_kernel(rel_ref, bkt_ref, o_ref):
    h = pl.program_id(0)
    bkt = bkt_ref[...]
    acc = jnp.full(bkt.shape, NEG, F32)
    for b in range(REL_BUCKETS):
        acc = jnp.where(bkt == b, rel_ref[b, h], acc)
    o_ref[0] = acc


def bias_lookup(rel_bias, dist, valid):
    bkt = jnp.where(jnp.asarray(valid), _rel_bucket(jnp.asarray(dist, jnp.int32)), -1)
    R, C = dist.shape
    H = rel_bias.shape[1]
    return pl.pallas_call(
        _bias_kernel,
        out_shape=jax.ShapeDtypeStruct((H, R, C), F32),
        grid=(H,),
        in_specs=[pl.BlockSpec(memory_space=pltpu.SMEM), pl.BlockSpec((R, C), lambda h: (0, 0))],
        out_specs=pl.BlockSpec((1, R, C), lambda h: (h, 0, 0)),
        compiler_params=_cparams(("parallel",)),
        name="bias_lookup",
    )(rel_bias, bkt)


def nsa_tables(rel_bias, S):
    T = LANES
    n_cmp = (S - CMP_BLOCK) // CMP_STRIDE + 1
    n_sblk = S // SEL_BLOCK
    nkb = S // T
    nwb = WINDOW // T + 1
    assert n_cmp < T and n_sblk <= T and S // CMP_STRIDE == T and nwb <= nkb
    ii = np.arange(T)[:, None]
    cc = np.arange(nkb * T)[None, :]
    dist = ii + T * (cc // T) - cc % T
    bias_s = bias_lookup(rel_bias, dist, dist >= 0)
    bias_w = bias_lookup(rel_bias, dist[:, :nwb * T], ((dist >= 0) & (dist < WINDOW))[:, :nwb * T])
    dist_c = np.arange(S)[:, None] - (CMP_STRIDE * np.arange(T) + CMP_BLOCK - 1)[None, :]
    bias_c = bias_lookup(rel_bias, dist_c, (dist_c >= 0) & (np.arange(T)[None, :] < n_cmp))

    cmp_start = CMP_STRIDE * np.arange(T)
    sel_start = SEL_BLOCK * np.arange(T)
    ov = ((cmp_start[:, None] <= (sel_start + SEL_BLOCK - 1)[None, :])
          & ((cmp_start + CMP_BLOCK - 1)[:, None] >= sel_start[None, :])
          & (np.arange(T)[:, None] < n_cmp) & (np.arange(T)[None, :] < n_sblk))
    spb = T // SEL_BLOCK
    c = np.arange(nkb * T)
    back = spb * (c // T) + spb - 1 - (c % T) // SEL_BLOCK
    expand = np.arange(T)[:, None] == back[None, :]
    return bias_c, bias_s, bias_w, jnp.asarray(ov.T, BF16), jnp.asarray(expand, BF16)


def nsa_attention(zm, zt, kcmp, vcmp, tables):
    B, S, _ = zm.shape
    G, HG, d, T = NSA_KV_HEADS, NSA_GROUP, NSA_HD, LANES
    assert G == 2 and d == T
    n_sblk = S // SEL_BLOCK
    nkb = S // T
    nwb = WINDOW // T + 1
    bias_c, bias_s, bias_w, ov_t, expand = tables

    qb = ZM_NSA_Q // (HG * d)
    kvb = ZM_NSA_KV // d
    kvspec = lambda off: pl.BlockSpec((1, S, d), lambda b, g, i: (b, 0, kvb + off * G + g))
    cspec = pl.BlockSpec((1, 1, T, d), lambda b, g, i: (b, g, 0, 0))
    flip = pltpu.VMEM(((2 * nkb - 1) * T, d), BF16)
    return pl.pallas_call(
        functools.partial(_nsa_kernel, n_sblk=n_sblk, nkb=nkb),
        out_shape=jax.ShapeDtypeStruct((B, S, NSA_Q), BF16),
        grid=(B, G, nkb),
        in_specs=[
            pl.BlockSpec((1, T, HG * d), lambda b, g, i: (b, i, qb + g)),
            pl.BlockSpec((1, T, NSA_GATE_PAD), lambda b, g, i: (b, i, ZT_GATE // NSA_GATE_PAD)),
            cspec, cspec, kvspec(2), kvspec(3), kvspec(4), kvspec(5),
            pl.BlockSpec((HG, T, T), lambda b, g, i: (g, i, 0)),
            pl.BlockSpec((HG, T, nkb * T), lambda b, g, i: (g, 0, 0)),
            pl.BlockSpec((HG, T, nwb * T), lambda b, g, i: (g, 0, 0)),
            pl.BlockSpec((T, T), lambda b, g, i: (0, 0)),
            pl.BlockSpec((T, nkb * T), lambda b, g, i: (0, 0)),
        ],
        out_specs=pl.BlockSpec((1, T, HG * d), lambda b, g, i: (b, i, g)),
        scratch_shapes=[pltpu.VMEM((HG * T, d), BF16), pltpu.VMEM((HG, T, d), F32), flip, flip, flip, flip],
        compiler_params=_cparams(("parallel", "parallel", "arbitrary")),
        name="nsa_attention",
    )(zm, zt, kcmp, vcmp, zm, zm, zm, zm, bias_c, bias_s, bias_w, ov_t, expand)


def _rwkv_prep_kernel(zr_ref, zk_ref, zv_ref, zl_ref, mu_ref, mul_ref, vec_ref, w2_ref, a2_ref, g2_ref,
                      r_ref, k_ref, v_ref, lw_ref, kk_ref, kka_ref, gg_ref, c_scr, cl_scr):
    first = pl.program_id(1) == 0
    tb = zr_ref.shape[1]
    row = lax.broadcasted_iota(jnp.int32, (tb, 1), 0)

    def shift(z, mu, carry_ref, slot):
        prev_last = jnp.where(first, 0.0, carry_ref[slot:slot + 1, :])
        prev = jnp.where(row == 0, prev_last, pltpu.roll(z, 1, 0))
        carry_ref[slot:slot + 1, :] = z[tb - 1:tb, :]
        return z + (prev - z) * mu

    r = shift(zr_ref[0], mu_ref[0:1, :], c_scr, 0)
    k = shift(zk_ref[0], mu_ref[1:2, :], c_scr, 1)
    v = shift(zv_ref[0], mu_ref[2:3, :], c_scr, 2)
    xl = shift(zl_ref[0], mul_ref[...], cl_scr, 0)
    w0, a0, k_k, k_a = vec_ref[0:1, :], vec_ref[1:2, :], vec_ref[2:3, :], vec_ref[3:4, :]
    u = -(w0 + _dot(jnp.tanh(xl), w2_ref[...]))
    w_log = -(jnp.maximum(u, 0.0) + jnp.log1p(jnp.exp(-jnp.abs(u)))) - 0.5
    a = jax.nn.sigmoid(a0 + _dot(xl, a2_ref[...]))
    r_ref[0] = r
    k_ref[0] = k * (1.0 + (a - 1.0) * k_a)
    v_ref[0] = v
    lw_ref[0] = -jnp.exp(w_log)
    kk = k * k_k
    kk_ref[0] = kk
    kka_ref[0] = kk * a
    gg_ref[0] = _dot(jax.nn.sigmoid(xl), g2_ref[...])


def rwkv_prep(z3, mu_rkv, mu_lora, vecs, w2p, a2p, g2p, tb):
    B, S, _ = z3.shape
    Dm = RWKV_DIM
    rb = ZT_RWKV // Dm
    zspec = lambda o: pl.BlockSpec((1, tb, Dm), lambda b, i: (b, i, rb + o))
    const = lambda shape: pl.BlockSpec(shape, lambda b, i: (0,) * len(shape))
    ospec = pl.BlockSpec((1, tb, Dm), lambda b, i: (b, i, 0))
    out = jax.ShapeDtypeStruct((B, S, Dm), F32)
    return pl.pallas_call(
        _rwkv_prep_kernel,
        out_shape=(out,) * 7,
        grid=(B, S // tb),
        in_specs=[zspec(0), zspec(1), zspec(2),
                  pl.BlockSpec((1, tb, LORA_PAD), lambda b, i: (b, i, ZT_LORA // LORA_PAD)),
                  const((3, Dm)), const((1, LORA_PAD)), const((4, Dm)),
                  const((LORA_PAD, Dm)), const((LORA_PAD, Dm)), const((LORA_PAD, Dm))],
        out_specs=(ospec,) * 7,
        scratch_shapes=[pltpu.VMEM((8, Dm), F32), pltpu.VMEM((8, LORA_PAD), F32)],
        compiler_params=_cparams(("parallel", "arbitrary")),
        name="rwkv_prep",
    )(z3, z3, z3, z3, mu_rkv, mu_lora, vecs, w2p, a2p, g2p)


def _rwkv_masks():
    T, n = RWKV_CHUNK, RWKV_GW
    idx = np.arange(n)
    h, t = idx // T, idx % T
    same = h[:, None] == h[None, :]
    tt, ss = t[:, None], t[None, :]
    levels = []
    b = 1
    while b < T:
        levels.append(same & (tt // (2 * b) == ss // (2 * b)) & (tt % (2 * b) >= b) & (ss % (2 * b) < b))
        b *= 2
    masks = [same, same & (ss < tt), same & (ss <= tt), np.eye(n, dtype=bool)] + levels
    return np.stack(masks).astype(np.float32), len(levels)


def _rwkv_chunk_kernel(r_ref, k_ref, v_ref, lw_ref, kk_ref, kka_ref, gg_ref, rk_ref, ln_ref, msk_ref, tri_ref,
                       o_ref, s_scr, *, n_levels):
    T, GW = RWKV_CHUNK, RWKV_GW

    @pl.when(pl.program_id(1) == 0)
    def _():
        s_scr[...] = jnp.zeros_like(s_scr)

    m_bd = msk_ref[0]
    m_strict = msk_ref[1]
    m_incl = msk_ref[2]
    eye = msk_ref[3].astype(F32)
    tri = tri_ref[...]

    def to_bd(x):
        return jnp.concatenate([x.astype(BF16)] * RWKV_GROUP, axis=0) * m_bd

    def from_bd(y):
        out = y[0:T]
        for i in range(1, RWKV_GROUP):
            out = out + y[i * T:(i + 1) * T]
        return out

    def seg_sums(xs):
        x = jnp.concatenate(xs, axis=0)
        hi = x.astype(BF16)
        lo = (x - hi.astype(F32)).astype(BF16)
        n = x.shape[0]
        both = mm(jnp.concatenate([hi, lo], axis=0), m_bd)
        out = both[0:n] + both[n:2 * n]
        return [out[i * T:(i + 1) * T] for i in range(len(xs))]

    def mm(a, b):
        return jnp.dot(a, b, preferred_element_type=F32)

    def mm_nt(a, b):
        return lax.dot_general(a, b, (((1,), (1,)), ((), ())), preferred_element_type=F32)

    def mm_tn(a, b):
        return lax.dot_general(a, b, (((0,), (0,)), ((), ())), preferred_element_type=F32)

    groups = range(RWKV_DIM // GW)
    sls = [slice(gi * GW, (gi + 1) * GW) for gi in groups]
    rs = [r_ref[0, :, sl] for sl in sls]
    ks = [k_ref[0, :, sl] for sl in sls]
    vs = [v_ref[0, :, sl] for sl in sls]
    lws = [lw_ref[0, :, sl] for sl in sls]
    sums = seg_sums([kk_ref[0, :, sl] * kk_ref[0, :, sl] for sl in sls]
                    + [r * k * rk_ref[:, sl] for r, k, sl in zip(rs, ks, sls)])
    inv_n = [1.0 / jnp.maximum(jnp.sqrt(ss), 1e-12) for ss in sums[:len(sls)]]
    bonus = [rk_sum * v for rk_sum, v in zip(sums[len(sls):], vs)]
    cum = [_split_dot_left(tri, lw) for lw in lws]
    p_in = [jnp.exp(c) for c in cum]
    p_inv = [jnp.exp(-c) for c in cum]
    a_bd = [to_bd(-(kk_ref[0, :, sl] * n) * jnp.exp(c - lw)) for sl, n, c, lw in zip(sls, inv_n, cum, lws)]
    r_bd = [to_bd(r * p) for r, p in zip(rs, p_in)]
    b_bd = [to_bd(kka_ref[0, :, sl] * n * p) for sl, n, p in zip(sls, inv_n, p_inv)]
    k_bd = [to_bd(k * p) for k, p in zip(ks, p_inv)]
    v_bd = [to_bd(v) for v in vs]

    ar_bd = [jnp.concatenate([a, r], axis=0) for a, r in zip(a_bd, r_bd)]
    on_b = [mm_nt(ar, b) for ar, b in zip(ar_bd, b_bd)]
    on_k = [mm_nt(ar, k) for ar, k in zip(ar_bd, k_bd)]
    a_ab = [(t[0:GW] * m_strict).astype(BF16) for t in on_b]
    a_rb = [(t[GW:2 * GW] * m_incl).astype(BF16) for t in on_b]
    a_k = [jnp.concatenate([(t[0:GW] * m_strict).astype(BF16), (t[GW:2 * GW] * m_incl).astype(BF16)], axis=0)
           for t in on_k]

    x = [eye + a * msk_ref[4] for a in a_ab]
    for lv in range(1, n_levels):
        xb = [xi.astype(BF16) for xi in x]
        t = [mm(xi, a * msk_ref[4 + lv]).astype(BF16) for xi, a in zip(xb, a_ab)]
        x = [xf + mm(ti, xi) for xf, ti, xi in zip(x, t, xb)]
    xb = [xi.astype(BF16) for xi in x]

    state = [s_scr[gi] for gi in groups]
    sb = [s.astype(BF16) for s in state]
    on_s = [mm_nt(ar, s) + mm(ak, v) for ar, s, ak, v in zip(ar_bd, sb, a_k, v_bd)]
    u = [mm(xi, t[0:GW].astype(BF16)).astype(BF16) for xi, t in zip(xb, on_s)]
    y = [from_bd(t[GW:2 * GW] + mm(arb, ui)) for t, arb, ui in zip(on_s, a_rb, u)]
    for gi in groups:
        s_scr[gi] = (state[gi] + mm_tn(u[gi], b_bd[gi]) + mm_tn(v_bd[gi], k_bd[gi])) * p_in[gi][T - 1:T, :]

    mu = [m * (1.0 / RWKV_HD) for m in seg_sums(y)]
    d = [yi - m for yi, m in zip(y, mu)]
    var = [s2 * (1.0 / RWKV_HD) for s2 in seg_sums([di * di for di in d])]
    for gi, sl in enumerate(sls):
        yn = d[gi] * lax.rsqrt(var[gi] + RWKV_LN_EPS) * ln_ref[0:1, sl] + ln_ref[1:2, sl]
        o_ref[0, :, sl] = ((yn + bonus[gi]) * gg_ref[0, :, sl]).astype(o_ref.dtype)


def _split_dot_left(w_bf16, x):
    acc = None
    rem = x
    for _ in range(3):
        hi = rem.astype(BF16)
        t = jnp.dot(w_bf16, hi, preferred_element_type=F32)
        acc = t if acc is None else acc + t
        rem = rem - hi.astype(F32)
    return acc


def rwkv_chunk(r, k, v, lw, kk, kka, gg, rk, ln):
    B, S, Dm = r.shape
    T = RWKV_CHUNK
    masks, n_levels = _rwkv_masks()
    tri = np.tril(np.ones((T, T), np.float32))
    ng = Dm // RWKV_GW
    xspec = pl.BlockSpec((1, T, Dm), lambda b, c: (b, c, 0))
    const = lambda shape: pl.BlockSpec(shape, lambda b, c: (0,) * len(shape))
    return pl.pallas_call(
        functools.partial(_rwkv_chunk_kernel, n_levels=n_levels),
        out_shape=jax.ShapeDtypeStruct((B, S, Dm), BF16),
        grid=(B, S // T),
        in_specs=[xspec] * 7 + [const((1, Dm)), const((2, Dm)), const(masks.shape), const((T, T))],
        out_specs=xspec,
        scratch_shapes=[pltpu.VMEM((ng, RWKV_GW, RWKV_GW), F32)],
        compiler_params=_cparams(("parallel", "arbitrary")),
        name="rwkv_chunk",
    )(r, k, v, lw, kk, kka, gg, rk, ln, jnp.asarray(masks, BF16), jnp.asarray(tri, BF16))


def _cast_kernel(x_ref, o_ref):
    o_ref[...] = x_ref[...].astype(o_ref.dtype)


def cast_leading_rows(wt, n_rows, tr):
    L, _, K = wt.shape
    spec = pl.BlockSpec((None, tr, K), lambda l, i: (l, i, 0))
    return pl.pallas_call(
        _cast_kernel,
        out_shape=jax.ShapeDtypeStruct((L, n_rows, K), BF16),
        grid=(L, n_rows // tr),
        in_specs=[spec],
        out_specs=spec,
        compiler_params=_cparams(("parallel", "parallel")),
        name="cast_w_main",
    )(wt)


def _tail_cast_kernel(a_ref, b_ref, o_ref, *, shift, n_real):
    j = pl.program_id(1)
    a = a_ref[...]
    b = b_ref[...]
    y = jnp.concatenate([a[shift:, :], b[:shift, :]], axis=0)
    row = lax.broadcasted_iota(jnp.int32, y.shape, 0)
    y = jnp.where(j * LANES + row < n_real, y, 0.0)
    lead = jnp.concatenate([jnp.zeros((LANES - shift, a.shape[1]), F32), a[:shift, :]], axis=0)
    o_ref[...] = jnp.where(j == pl.num_programs(1) - 1, lead, y).astype(o_ref.dtype)


def cast_tail_rows(wt, row0):
    L, n_all, K = wt.shape
    shift = 3 * NSA_HEADS
    n_real = n_all - row0 - shift
    nblk = ZT_COLS // LANES
    assert row0 % LANES == 0 and shift % 8 == 0 and n_real == 3 * RWKV_DIM + LORA_COLS and n_real <= ZT_GATE
    b0 = row0 // LANES
    last = (n_all - 1) // LANES
    return pl.pallas_call(
        functools.partial(_tail_cast_kernel, shift=shift, n_real=n_real),
        out_shape=jax.ShapeDtypeStruct((L, ZT_COLS, K), BF16),
        grid=(L, nblk),
        in_specs=[
            pl.BlockSpec((None, LANES, K), lambda l, j: (l, jnp.where(j == nblk - 1, b0, jnp.minimum(b0 + j, last)), 0)),
            pl.BlockSpec((None, LANES, K), lambda l, j: (l, jnp.minimum(b0 + j + 1, last), 0)),
        ],
        out_specs=pl.BlockSpec((None, LANES, K), lambda l, j: (l, j, 0)),
        compiler_params=_cparams(("parallel", "parallel")),
        name="cast_w_tail",
    )(wt, wt)


def _prep_weights(w_in, rwkv_mu, rwkv_w2, rwkv_a2, rwkv_g2):
    L, D, _ = w_in.shape
    wt = jnp.swapaxes(w_in, 1, 2)
    o = 3 * D + ZM_COLS
    w_main = cast_leading_rows(wt, o, 512)
    w_tail = cast_tail_rows(wt, o)
    mu_rkv = rwkv_mu[:, :3 * RWKV_DIM].reshape(L, 3, RWKV_DIM)
    mu_lora = jnp.pad(rwkv_mu[:, 3 * RWKV_DIM:], ((0, 0), (0, LORA_PAD - LORA_COLS))).reshape(L, 1, LORA_PAD)
    rows = lambda w, start: jnp.pad(w, ((0, 0), (start, LORA_PAD - start - w.shape[1]), (0, 0)))
    w2p = rows(rwkv_w2, 0)
    a2p = rows(rwkv_a2, DECAY_LORA)
    g2p = rows(rwkv_g2, DECAY_LORA + AAA_LORA)
    return w_main, w_tail, mu_rkv, mu_lora, w2p, a2p, g2p


def kernel(x, c, rel_bias, w_in, w_branch_ret, w_branch_nsa, w_branch_rwkv, w_out, ffn1_in, ffn1_out, ffn2_in,
           ffn2_out, ada_down, ada_up, ada_bias, norm_pre, norm_post, cmp_pos, cmp_w1, cmp_b1, cmp_w2, cmp_b2,
           rwkv_mu, rwkv_vecs, rwkv_w2, rwkv_a2, rwkv_g2, rwkv_rk, rwkv_ln):
    B, S, D = x.shape
    L = w_in.shape[0]
    M = B * S
    tm = min(1024, M)
    ts = min(256, S)
    tb = min(256, S)
    tn_gate = _col_tile(3 * D, 1024)
    tn_mix = _col_tile(math.gcd(3 * D, ZM_COLS), 1024)
    tn_tail = _col_tile(ZT_COLS, 512)
    tm_post = min(512, S)
    tn_post = max(D // 8, LANES)

    w_main, w_tail, mu_rkv, mu_lora, w2p, a2p, g2p = _prep_weights(w_in, rwkv_mu, rwkv_w2, rwkv_a2, rwkv_g2)
    tables = nsa_tables(rel_bias, S)
    w_bret, w_bnsa, w_brwkv = w_branch_ret.astype(BF16), w_branch_nsa.astype(BF16), w_branch_rwkv.astype(BF16)
    w_o = w_out.astype(BF16)
    f1i, f1o, f2i, f2o = (w.astype(BF16) for w in (ffn1_in, ffn1_out, ffn2_in, ffn2_out))
    npre = norm_pre.reshape(L, N_SUB, 1, D)
    npost = norm_post.reshape(L, N_SUB, 1, D)
    rk = rwkv_rk.reshape(L, 1, RWKV_DIM)

    mod = ada_mod(c, ada_down, ada_up, ada_bias)
    h = norm_modulate(x, npre, mod, 0, 0, ts)
    for l in range(L):
        u = swiglu_in(h.reshape(M, D), f1i, l, tm, 512)
        x, h = matmul_post(u, f1o, x, npost, npre, mod, l, 0, 0.5, (l, 1), tm_post, tn_post)
        h2 = h.reshape(M, D)
        zg = matmul(h2, w_main, l, tm, tn_gate, BF16, act="sigmoid", name="gate_proj", n_cols=3 * D,
                    w_rows_are_outputs=True)
        zm = matmul(h2, w_main, l, tm, tn_mix, F32, name="mix_proj", col0=3 * D, n_cols=ZM_COLS,
                    w_rows_are_outputs=True).reshape(B, S, ZM_COLS)
        zt = matmul(h2, w_tail, l, tm, tn_tail, F32, name="tail_proj", w_rows_are_outputs=True)
        zt = zt.reshape(B, S, ZT_COLS)
        o_ret = retention(zm)
        kcmp, vcmp = nsa_compress(zm, cmp_pos[l], cmp_w1[l], cmp_b1[l], cmp_w2[l], cmp_b2[l])
        o_nsa = nsa_attention(zm, zt, kcmp, vcmp, tables)
        rw = rwkv_prep(zt, mu_rkv[l], mu_lora[l], rwkv_vecs[l], w2p[l], a2p[l], g2p[l], tb)
        o_rwkv = rwkv_chunk(*rw, rk[l], rwkv_ln[l])
        merged = branch_merge(o_ret.reshape(M, RET_V), o_nsa.reshape(M, NSA_Q), o_rwkv.reshape(M, RWKV_DIM),
                              w_bret, w_bnsa, w_brwkv, zg, l, tm, 512)
        x, h = matmul_post(merged, w_o, x, npost, npre, mod, l, 1, 1.0, (l, 2), tm_post, tn_post)
        u = swiglu_in(h.reshape(M, D), f2i, l, tm, 512)
        nxt = (l + 1, 0) if l + 1 < L else None
        x, h = matmul_post(u, f2o, x, npost, npre, mod, l, 2, 0.5, nxt, tm_post, tn_post)
    return x
```
